```python
import jax
import jax.numpy as jnp
from jax import lax
import numpy as np

D_MODEL = 2048
BATCH = 4
SEQ = 4096
DEPTH = 4

N_META = 16
EPS = 1e-6
NEG = -1e30
ATT_BLOCK = 128

SWA_HEADS = 16
SWA_KV_HEADS = 4
SWA_HEAD_DIM = 64
SWA_WINDOW = 128

MLA_HEADS = 16
MLA_Q_RANK = 512
MLA_KV_RANK = 256
MLA_NOPE_DIM = 64
MLA_ROPE_DIM = 32
MLA_V_DIM = 64
ROPE_THETA = 10000.0

REC_EXPAND = 128
REC_HEADS = D_MODEL // REC_EXPAND
REC_VDIM = D_MODEL // REC_HEADS
REC_CHUNK = 64

D_FF = 5632
CONV_WIDTH = 3

N_ATT_LAYERS = (DEPTH + 1) // 2
N_REC_LAYERS = DEPTH // 2

SWA_Q = SWA_HEADS * SWA_HEAD_DIM
SWA_KV = SWA_KV_HEADS * SWA_HEAD_DIM
ATT_IN = SWA_Q + 2 * SWA_KV + MLA_Q_RANK + MLA_KV_RANK + MLA_ROPE_DIM
ATT_OUT = SWA_Q + MLA_HEADS * MLA_V_DIM
REC_K = REC_HEADS * REC_EXPAND
REC_V = REC_HEADS * REC_VDIM
REC_IN = 2 * REC_K + 2 * REC_V

kernel_name = 'hybrid_swa_mla_hgrn2_convffn_meta'


def rms_norm(x, g):
    xf = x.astype(jnp.float32)
    y = xf * lax.rsqrt(jnp.mean(xf * xf, axis=-1, keepdims=True) + EPS)
    return (y * g.astype(jnp.float32)).astype(x.dtype)


def pad_front(a, block):
    pad = (-N_META) % block
    widths = [(0, 0), (pad, 0)] + [(0, 0)] * (a.ndim - 2)
    return jnp.pad(a, widths), pad


def apply_rope(x, cos, sin):
    x1, x2 = jnp.split(x.astype(jnp.float32), 2, axis=-1)
    shape = (cos.shape[0],) + (1,) * (x.ndim - 3) + (cos.shape[1],)
    c, s = cos.reshape(shape), sin.reshape(shape)
    return jnp.concatenate([x1 * c - x2 * s, x2 * c + x1 * s], axis=-1).astype(x.dtype)


def sliding_window_sink_attention(q, k, v, sinks):
    B = q.shape[0]
    G = SWA_HEADS // SWA_KV_HEADS
    blk = ATT_BLOCK
    qp, pad = pad_front(q, blk)
    kp, _ = pad_front(k, blk)
    vp, _ = pad_front(v, blk)
    T = qp.shape[1]
    NB = T // blk
    qb = qp.reshape(B, NB, blk, SWA_KV_HEADS, G, SWA_HEAD_DIM)

    def band(a):
        ab = a.reshape(B, NB, blk, SWA_KV_HEADS, SWA_HEAD_DIM)
        prev = jnp.concatenate([jnp.zeros_like(ab[:, :1]), ab[:, :-1]], axis=1)
        return jnp.concatenate([prev, ab], axis=2)

    kb, vb = band(kp), band(vp)
    s = jnp.einsum('bnqkgd,bnskd->bnkgqs', qb, kb, preferred_element_type=jnp.float32) * (SWA_HEAD_DIM ** -0.5)
    qi = jnp.arange(blk)[:, None]
    kj = jnp.arange(2 * blk)[None, :]
    rel = qi + blk - kj
    in_window = (rel >= 0) & (rel < SWA_WINDOW)
    key_pos = jnp.arange(NB)[:, None] * blk + jnp.arange(2 * blk)[None, :] - blk
    key_ok = key_pos >= pad
    mask = in_window[None] & key_ok[:, None, :]
    s = jnp.where(mask[None, :, None, None], s, NEG)
    sink = sinks.astype(jnp.float32).reshape(SWA_KV_HEADS, G)[None, None, :, :, None, None]
    m = jnp.maximum(jnp.max(s, axis=-1, keepdims=True), sink)
    p = jnp.exp(s - m)
    p = p / (jnp.sum(p, axis=-1, keepdims=True) + jnp.exp(sink - m))
    o = jnp.einsum('bnkgqs,bnskd->bnqkgd', p.astype(vb.dtype), vb)
    return o.reshape(B, T, SWA_Q)[:, pad:]


def latent_attention(c_q, c_kv, k_rope, q_norm, w_uq, kv_norm, w_ukv, cos, sin):
    B, L = c_q.shape[:2]
    blk = ATT_BLOCK
    q = (rms_norm(c_q, q_norm) @ w_uq).reshape(B, L, MLA_HEADS, MLA_NOPE_DIM + MLA_ROPE_DIM)
    q_nope = q[..., :MLA_NOPE_DIM]
    q_rope = apply_rope(q[..., MLA_NOPE_DIM:], cos, sin)
    kv = (rms_norm(c_kv, kv_norm) @ w_ukv).reshape(B, L, MLA_HEADS, MLA_NOPE_DIM + MLA_V_DIM)
    k_nope = kv[..., :MLA_NOPE_DIM]
    v = kv[..., MLA_NOPE_DIM:]
    k_rope = apply_rope(k_rope, cos, sin)
    q_nope, pad = pad_front(q_nope, blk)
    q_rope, _ = pad_front(q_rope, blk)
    k_nope, _ = pad_front(k_nope, blk)
    k_rope, _ = pad_front(k_rope, blk)
    v, _ = pad_front(v, blk)
    T = q_nope.shape[1]
    NB = T // blk
    scale = (MLA_NOPE_DIM + MLA_ROPE_DIM) ** -0.5
    key_pos = jnp.arange(T)

    def query_block(args):
        qn, qr, n = args
        s = (jnp.einsum('bqhd,bshd->bhqs', qn, k_nope, preferred_element_type=jnp.float32)
             + jnp.einsum('bqhd,bsd->bhqs', qr, k_rope, preferred_element_type=jnp.float32)) * scale
        q_pos = n * blk + jnp.arange(blk)
        mask = (key_pos[None, :] <= q_pos[:, None]) & (key_pos[None, :] >= pad)
        p = jax.nn.softmax(jnp.where(mask, s, NEG), axis=-1)
        return jnp.einsum('bhqs,bshd->bqhd', p.astype(v.dtype), v)

    qn_b = q_nope.reshape(B, NB, blk, MLA_HEADS, MLA_NOPE_DIM).transpose(1, 0, 2, 3, 4)
    qr_b = q_rope.reshape(B, NB, blk, MLA_HEADS, MLA_ROPE_DIM).transpose(1, 0, 2, 3, 4)
    o = lax.map(query_block, (qn_b, qr_b, jnp.arange(NB)))
    return o.transpose(1, 0, 2, 3, 4).reshape(B, T, MLA_HEADS * MLA_V_DIM)[:, pad:]


def attention_mixer(h, w_in, sinks, q_norm, w_uq, kv_norm, w_ukv, w_out, cos, sin):
    B, L, _ = h.shape
    z = h @ w_in
    cuts = [SWA_Q, SWA_Q + SWA_KV, SWA_Q + 2 * SWA_KV, SWA_Q + 2 * SWA_KV + MLA_Q_RANK,
            SWA_Q + 2 * SWA_KV + MLA_Q_RANK + MLA_KV_RANK]
    q_a, k_a, v_a, c_q, c_kv, k_r = jnp.split(z, cuts, axis=-1)
    o_a = sliding_window_sink_attention(
        q_a.reshape(B, L, SWA_HEADS, SWA_HEAD_DIM),
        k_a.reshape(B, L, SWA_KV_HEADS, SWA_HEAD_DIM),
        v_a.reshape(B, L, SWA_KV_HEADS, SWA_HEAD_DIM), sinks)
    o_b = latent_attention(c_q, c_kv, k_r, q_norm, w_uq, kv_norm, w_ukv, cos, sin)
    return jnp.concatenate([o_a, o_b], axis=-1) @ w_out


def hgrn2_mixer(h, w_in, lower_bound, out_norm, w_out):
    B, L, _ = h.shape
    z = h @ w_in
    q, f, i, g = jnp.split(z, [REC_K, 2 * REC_K, 2 * REC_K + REC_V], axis=-1)
    lb = lower_bound.astype(jnp.float32)
    log_f = jnp.logaddexp(jnp.log(lb), jnp.log1p(-lb) + jax.nn.log_sigmoid(f.astype(jnp.float32)))
    k = 1.0 - jnp.exp(log_f)
    q = jax.nn.silu(q.astype(jnp.float32))

    def heads(a, d):
        return a.reshape(B, L, REC_HEADS, d)

    qp, pad = pad_front(heads(q, REC_EXPAND), REC_CHUNK)
    kp, _ = pad_front(heads(k, REC_EXPAND), REC_CHUNK)
    gp, _ = pad_front(heads(log_f, REC_EXPAND), REC_CHUNK)
    vp, _ = pad_front(heads(i.astype(jnp.float32), REC_VDIM), REC_CHUNK)
    T = qp.shape[1]
    NC = T // REC_CHUNK

    def to_chunks(a):
        return a.reshape(B, NC, REC_CHUNK, REC_HEADS, a.shape[-1]).transpose(1, 0, 3, 2, 4)

    causal = jnp.tril(jnp.ones((REC_CHUNK, REC_CHUNK), dtype=bool))[:, :, None]

    def chunk_step(S, inp):
        qc, kc, vc, gc = inp
        b = jnp.cumsum(gc, axis=2)
        o_inter = jnp.einsum('bhtk,bhkv->bhtv', qc * jnp.exp(b), S)
        diff = b[:, :, :, None, :] - b[:, :, None, :, :]
        decay = jnp.exp(jnp.where(causal, diff, -jnp.inf))
        att = jnp.einsum('bhtk,bhtsk,bhsk->bhts', qc, decay, kc)
        o = o_inter + jnp.einsum('bhts,bhsv->bhtv', att, vc)
        b_last = b[:, :, -1:, :]
        S = (jnp.exp(b_last[:, :, 0, :])[..., None] * S
             + jnp.einsum('bhsk,bhsv->bhkv', kc * jnp.exp(b_last - b), vc))
        return S, o

    S0 = jnp.zeros((B, REC_HEADS, REC_EXPAND, REC_VDIM), jnp.float32)
    _, o = lax.scan(chunk_step, S0, (to_chunks(qp), to_chunks(kp), to_chunks(vp), to_chunks(gp)))
    o = o.transpose(1, 0, 3, 2, 4).reshape(B, T, REC_HEADS, REC_VDIM)[:, pad:]
    o = rms_norm(o, out_norm) * jax.nn.silu(heads(g.astype(jnp.float32), REC_VDIM))
    return o.reshape(B, L, REC_V).astype(h.dtype) @ w_out


def conv_ffn(h, w_up, w_gate, conv_w, conv_b, w_down):
    u = h @ w_up
    a = h @ w_gate
    a = lax.conv_general_dilated(
        a, conv_w[:, None, :], window_strides=(1,), padding=[(CONV_WIDTH - 1, 0)],
        dimension_numbers=('NWC', 'WIO', 'NWC'), feature_group_count=D_FF) + conv_b
    return (jax.nn.silu(a) * u) @ w_down


def setup_inputs(seed: int = 0) -> dict:
    key = jax.random.key(seed)
    ks = jax.random.split(key, 24)
    f32 = jnp.float32

    def w(k, shape, fan_in):
        return jax.random.normal(k, shape, f32) * (fan_in ** -0.5)

    def gain(k, shape):
        return 1.0 + 0.02 * jax.random.normal(k, shape, f32)

    NA, NR = N_ATT_LAYERS, N_REC_LAYERS
    return {
        'x': jax.random.normal(ks[0], (BATCH, SEQ, D_MODEL), f32),
        'meta_tokens': jax.random.normal(ks[1], (N_META, D_MODEL), f32),
        'mix_norm': gain(ks[2], (DEPTH, D_MODEL)),
        'ffn_norm': gain(ks[3], (DEPTH, D_MODEL)),
        'final_norm': gain(ks[4], (D_MODEL,)),
        'att_w_in': w(ks[5], (NA, D_MODEL, ATT_IN), D_MODEL),
        'att_sinks': 0.5 * jax.random.normal(ks[6], (NA, SWA_HEADS), f32),
        'mla_q_norm': gain(ks[7], (NA, MLA_Q_RANK)),
        'mla_w_uq': w(ks[8], (NA, MLA_Q_RANK, MLA_HEADS * (MLA_NOPE_DIM + MLA_ROPE_DIM)), MLA_Q_RANK),
        'mla_kv_norm': gain(ks[9], (NA, MLA_KV_RANK)),
        'mla_w_ukv': w(ks[10], (NA, MLA_KV_RANK, MLA_HEADS * (MLA_NOPE_DIM + MLA_V_DIM)), MLA_KV_RANK),
        'att_w_out': w(ks[11], (NA, ATT_OUT, D_MODEL), ATT_OUT),
        'rec_w_in': w(ks[12], (NR, D_MODEL, REC_IN), D_MODEL),
        'rec_lower_bounds': jax.random.normal(ks[13], (NR, REC_K), f32),
        'rec_out_norm': gain(ks[14], (NR, REC_VDIM)),
        'rec_w_out': w(ks[15], (NR, REC_V, D_MODEL), REC_V),
        'ffn_w_up': w(ks[16], (DEPTH, D_MODEL, D_FF), D_MODEL),
        'ffn_w_gate': w(ks[17], (DEPTH, D_MODEL, D_FF), D_MODEL),
        'ffn_conv_w': w(ks[18], (DEPTH, CONV_WIDTH, D_FF), CONV_WIDTH),
        'ffn_conv_b': 0.01 * jax.random.normal(ks[19], (DEPTH, D_FF), f32),
        'ffn_w_down': w(ks[20], (DEPTH, D_FF, D_MODEL), D_FF),
    }


def reference(x, meta_tokens, mix_norm, ffn_norm, final_norm, att_w_in, att_sinks, mla_q_norm,
              mla_w_uq, mla_kv_norm, mla_w_ukv, att_w_out, rec_w_in, rec_lower_bounds, rec_out_norm,
              rec_w_out, ffn_w_up, ffn_w_gate, ffn_conv_w, ffn_conv_b, ffn_w_down):
    B = x.shape[0]
    meta = jnp.broadcast_to(meta_tokens[None].astype(x.dtype), (B, N_META, D_MODEL))
    hs = jnp.concatenate([meta, x], axis=1)
    L = hs.shape[1]
    half = MLA_ROPE_DIM // 2
    inv_freq = ROPE_THETA ** (-2.0 * jnp.arange(half, dtype=jnp.float32) / MLA_ROPE_DIM)
    ang = jnp.arange(L).astype(jnp.float32)[:, None] * inv_freq[None, :]
    cos, sin = jnp.cos(ang), jnp.sin(ang)
    sm = jax.nn.softmax(rec_lower_bounds.astype(jnp.float32), axis=0)
    lower = jnp.cumsum(sm.at[0].set(0.0), axis=0)
    for layer in range(DEPTH):
        h = rms_norm(hs, mix_norm[layer])
        if layer % 2 == 0:
            a = layer // 2
            hs = hs + attention_mixer(h, att_w_in[a], att_sinks[a], mla_q_norm[a], mla_w_uq[a],
                                      mla_kv_norm[a], mla_w_ukv[a], att_w_out[a], cos, sin)
        else:
            r = layer // 2
            hs = hs + hgrn2_mixer(h, rec_w_in[r], lower[r], rec_out_norm[r], rec_w_out[r])
        h = rms_norm(hs, ffn_norm[layer])
        hs = hs + conv_ffn(h, ffn_w_up[layer], ffn_w_gate[layer], ffn_conv_w[layer],
                           ffn_conv_b[layer], ffn_w_down[layer])
    return rms_norm(hs, final_norm)[:, N_META:]
```

```python
import functools

import jax
import jax.numpy as jnp
from jax import lax
from jax.experimental import pallas as pl
from jax.experimental.pallas import tpu as pltpu

F32 = jnp.float32
BF16 = jnp.bfloat16

N_META = 16
EPS = 1e-6
NEG = -1e30
ATT_BLOCK = 128
SWA_HEADS = 16
SWA_KV_HEADS = 4
SWA_HEAD_DIM = 64
MLA_HEADS = 16
MLA_Q_RANK = 512
MLA_KV_RANK = 256
MLA_NOPE_DIM = 64
MLA_ROPE_DIM = 32
MLA_V_DIM = 64
ROPE_THETA = 10000.0
REC_HEADS = 16
REC_DIM = 128
CONV_WIDTH = 3

LANES = 128
BF16_ROWS = 16
FRAME_PAD = (-N_META) % ATT_BLOCK
REC_CHUNK = 64
REC_SUB = 16
VMEM_LIMIT = 56 * 1024 * 1024

SWA_Q = SWA_HEADS * SWA_HEAD_DIM
SWA_KV = SWA_KV_HEADS * SWA_HEAD_DIM
COL_QA = 0
COL_KA = SWA_Q
COL_VA = SWA_Q + SWA_KV
COL_CQ = SWA_Q + 2 * SWA_KV
COL_CKV = COL_CQ + MLA_Q_RANK
COL_KR = COL_CKV + MLA_KV_RANK
COL_KRROT = COL_KR + LANES
ATT_IN_PAD = COL_KRROT + LANES


def _pick(n, candidates):
    for c in candidates:
        if n % c == 0:
            return c
    raise ValueError(f"no block size in {candidates} divides {n}")


def _params(sem):
    return pltpu.CompilerParams(dimension_semantics=sem, vmem_limit_bytes=VMEM_LIMIT)


def _silu(x):
    return x * (1.0 / (1.0 + jnp.exp(-x)))


def _rmsnorm_body(x_ref, g_ref, o_ref):
    x = x_ref[...]
    ms = jnp.mean(x * x, axis=-1, keepdims=True)
    o_ref[...] = (x * lax.rsqrt(ms + EPS) * g_ref[...]).astype(o_ref.dtype)


def _rmsnorm(x, gain, out_dtype):
    R, D = x.shape
    bm = _pick(R, (768, 512, 384, 256, 128))
    return pl.pallas_call(
        _rmsnorm_body,
        grid=(R // bm,),
        in_specs=[pl.BlockSpec((bm, D), lambda i: (i, 0)),
                  pl.BlockSpec((1, D), lambda i: (0, 0))],
        out_specs=pl.BlockSpec((bm, D), lambda i: (i, 0)),
        out_shape=jax.ShapeDtypeStruct((R, D), out_dtype),
        compiler_params=_params(("parallel",)),
        name="rmsnorm",
    )(x, gain.reshape(1, D))


def _matmul_body(x_ref, w_ref, o_ref):
    o_ref[...] = jnp.dot(x_ref[...], w_ref[...], preferred_element_type=F32).astype(o_ref.dtype)


def _matmul(x, w, out_dtype):
    R, K = x.shape
    N = w.shape[1]
    bm = _pick(R, (1408, 768, 512, 384, 256, 128))
    bn = _pick(N, (1280, 1024, 512, 256, 128))
    return pl.pallas_call(
        _matmul_body,
        grid=(R // bm, N // bn),
        in_specs=[pl.BlockSpec((bm, K), lambda i, j: (i, 0)),
                  pl.BlockSpec((K, bn), lambda i, j: (0, j))],
        out_specs=pl.BlockSpec((bm, bn), lambda i, j: (i, j)),
        out_shape=jax.ShapeDtypeStruct((R, N), out_dtype),
        compiler_params=_params(("parallel", "arbitrary")),
        name="att_in_proj",
    )(x, w)


def _proj_res_body(x_ref, w_ref, res_ref, g_ref, *rest):
    *hs_refs, h_ref, acc_ref = rest
    k = pl.program_id(1)
    d = jnp.dot(x_ref[...], w_ref[...], preferred_element_type=F32)

    @pl.when(k == 0)
    def _():
        acc_ref[...] = d

    @pl.when(k > 0)
    def _():
        acc_ref[...] += d

    @pl.when(k == pl.num_programs(1) - 1)
    def _():
        y = res_ref[...] + acc_ref[...]
        for hs_ref in hs_refs:
            hs_ref[...] = y
        ms = jnp.mean(y * y, axis=-1, keepdims=True)
        h_ref[...] = (y * lax.rsqrt(ms + EPS) * g_ref[...]).astype(h_ref.dtype)


def _proj_residual_norm(x, w, res, gain, h_dtype, emit_residual=True):
    R, K = x.shape
    N = w.shape[1]
    bm = _pick(R, (768, 512, 384, 256, 128))
    bk = _pick(K, (512, 256, 128))
    row_spec = pl.BlockSpec((bm, N), lambda i, k: (i, 0))
    out_specs = [row_spec]
    out_shape = [jax.ShapeDtypeStruct((R, N), h_dtype)]
    if emit_residual:
        out_specs = [row_spec] + out_specs
        out_shape = [jax.ShapeDtypeStruct((R, N), F32)] + out_shape
    return pl.pallas_call(
        _proj_res_body,
        grid=(R // bm, K // bk),
        in_specs=[pl.BlockSpec((bm, bk), lambda i, k: (i, k)),
                  pl.BlockSpec((bk, N), lambda i, k: (k, 0)),
                  row_spec,
                  pl.BlockSpec((1, N), lambda i, k: (0, 0))],
        out_specs=out_specs,
        out_shape=out_shape,
        scratch_shapes=[pltpu.VMEM((bm, N), F32)],
        compiler_params=_params(("parallel", "arbitrary")),
        name="proj_residual_norm",
    )(x, w, res, gain.reshape(1, N))


def _ffn_up_body(x_ref, halo_ref, wu_ref, wg_ref, cw_ref, cb_ref, o_ref, lhs_ref, a_ref):
    bm = x_ref.shape[0]
    halo = halo_ref.shape[0]

    @pl.when(pl.program_id(1) == 0)
    def _():
        lhs_ref[0:halo, :] = halo_ref[...]
        lhs_ref[halo:, :] = x_ref[...]

    a_ref[...] = jnp.dot(lhs_ref[...], wg_ref[...], preferred_element_type=F32)
    u = jnp.dot(x_ref[...], wu_ref[...], preferred_element_type=F32)
    cw = cw_ref[...]
    a = cb_ref[...] + cw[CONV_WIDTH - 1:CONV_WIDTH, :] * a_ref[halo:, :]
    for j in range(CONV_WIDTH - 1):
        back = CONV_WIDTH - 1 - j
        a = a + cw[j:j + 1, :] * a_ref[pl.ds(halo - back, bm), :]
    o_ref[...] = (_silu(a) * u).astype(o_ref.dtype)


def _ffn_up(h, w_up, w_gate, conv_w, conv_b):
    R, K = h.shape
    N = w_up.shape[1]
    bm = _pick(R, (1408, 768, 512, 384, 256, 128))
    bn = _pick(N, (512, 256, 128))
    halo = BF16_ROWS
    hb = bm // halo
    return pl.pallas_call(
        _ffn_up_body,
        grid=(R // bm, N // bn),
        in_specs=[pl.BlockSpec((bm, K), lambda i, j: (i, 0)),
                  pl.BlockSpec((halo, K), lambda i, j: (jnp.maximum(i * hb - 1, 0), 0)),
                  pl.BlockSpec((K, bn), lambda i, j: (0, j)),
                  pl.BlockSpec((K, bn), lambda i, j: (0, j)),
                  pl.BlockSpec((CONV_WIDTH, bn), lambda i, j: (0, j)),
                  pl.BlockSpec((1, bn), lambda i, j: (0, j))],
        out_specs=pl.BlockSpec((bm, bn), lambda i, j: (i, j)),
        out_shape=jax.ShapeDtypeStruct((R, N), BF16),
        scratch_shapes=[pltpu.VMEM((bm + halo, K), BF16),
                        pltpu.VMEM((bm + halo, bn), F32)],
        compiler_params=_params(("parallel", "arbitrary")),
        name="ffn_up",
    )(h, h, w_up, w_gate, conv_w, conv_b.reshape(1, N))


def _swa_body(sink_ref, q_ref, kp_ref, kc_ref, vp_ref, vc_ref, o_ref):
    n = pl.program_id(1)
    blk = q_ref.shape[0]
    dh = SWA_HEAD_DIM
    group = SWA_HEADS // SWA_KV_HEADS
    qi = lax.broadcasted_iota(jnp.int32, (blk, 2 * blk), 0)
    kj = lax.broadcasted_iota(jnp.int32, (blk, 2 * blk), 1)
    rel = qi + blk - kj
    key_pos = n * blk + kj - blk
    mask = (rel >= 0) & (rel < blk) & (key_pos >= FRAME_PAD)
    for g in range(SWA_KV_HEADS):
        ks = slice(g * dh, (g + 1) * dh)
        k = jnp.concatenate([kp_ref[:, ks], kc_ref[:, ks]], axis=0)
        v = jnp.concatenate([vp_ref[:, ks], vc_ref[:, ks]], axis=0)
        for hh in range(group):
            h = g * group + hh
            hs = slice(h * dh, (h + 1) * dh)
            q = q_ref[:, hs] * (dh ** -0.5)
            s = lax.dot_general(q, k, (((1,), (1,)), ((), ())), preferred_element_type=F32)
            s = jnp.where(mask, s, NEG)
            sink = sink_ref[h]
            m = jnp.maximum(jnp.max(s, axis=-1, keepdims=True), sink)
            p = jnp.exp(s - m)
            den = jnp.sum(p, axis=-1, keepdims=True) + jnp.exp(sink - m)
            o = jnp.dot(p.astype(BF16), v, preferred_element_type=F32) / den
            o_ref[:, hs] = o.astype(o_ref.dtype)


def _swa_attention(z, sinks, batch, n_blocks):
    R = z.shape[0]
    blk = ATT_BLOCK
    kcol = COL_KA // SWA_KV
    vcol = COL_VA // SWA_KV

    def cur(b, n):
        return b * n_blocks + n

    def prev(b, n):
        return b * n_blocks + jnp.maximum(n - 1, 0)

    return pl.pallas_call(
        _swa_body,
        grid=(batch, n_blocks),
        in_specs=[pl.BlockSpec(memory_space=pltpu.SMEM),
                  pl.BlockSpec((blk, SWA_Q), lambda b, n: (cur(b, n), 0)),
                  pl.BlockSpec((blk, SWA_KV), lambda b, n: (prev(b, n), kcol)),
                  pl.BlockSpec((blk, SWA_KV), lambda b, n: (cur(b, n), kcol)),
                  pl.BlockSpec((blk, SWA_KV), lambda b, n: (prev(b, n), vcol)),
                  pl.BlockSpec((blk, SWA_KV), lambda b, n: (cur(b, n), vcol))],
        out_specs=pl.BlockSpec((blk, SWA_Q), lambda b, n: (cur(b, n), 0)),
        out_shape=jax.ShapeDtypeStruct((R, SWA_Q), BF16),
        compiler_params=_params(("parallel", "arbitrary")),
        name="swa_attention",
    )(sinks, z, z, z, z, z)


def _mla_q_body(cq_ref, g_ref, w_ref, wrot_ref, cos_ref, sin_ref, o_ref):
    x = cq_ref[...].astype(F32)
    ms = jnp.mean(x * x, axis=-1, keepdims=True)
    h = (x * lax.rsqrt(ms + EPS) * g_ref[...]).astype(BF16)
    a = jnp.dot(h, w_ref[...], preferred_element_type=F32)
    ar = jnp.dot(h, wrot_ref[...], preferred_element_type=F32)
    c = cos_ref[...]
    s = sin_ref[...]
    for hh in range(MLA_HEADS):
        sl = slice(hh * LANES, (hh + 1) * LANES)
        o_ref[:, sl] = (a[:, sl] * c + ar[:, sl] * s).astype(o_ref.dtype)


def _mla_q_proj(z, gain, w, wrot, cos_t, sin_t, t_blocks, bm):
    R = z.shape[0]
    N = w.shape[1]
    return pl.pallas_call(
        _mla_q_body,
        grid=(R // bm,),
        in_specs=[pl.BlockSpec((bm, MLA_Q_RANK), lambda i: (i, COL_CQ // MLA_Q_RANK)),
                  pl.BlockSpec((1, MLA_Q_RANK), lambda i: (0, 0)),
                  pl.BlockSpec((MLA_Q_RANK, N), lambda i: (0, 0)),
                  pl.BlockSpec((MLA_Q_RANK, N), lambda i: (0, 0)),
                  pl.BlockSpec((bm, LANES), lambda i: (i % t_blocks, 0)),
                  pl.BlockSpec((bm, LANES), lambda i: (i % t_blocks, 0))],
        out_specs=pl.BlockSpec((bm, N), lambda i: (i, 0)),
        out_shape=jax.ShapeDtypeStruct((R, N), BF16),
        compiler_params=_params(("parallel",)),
        name="mla_q_proj",
    )(z, gain.reshape(1, -1), w, wrot, cos_t, sin_t)


def _mla_kv_body(ckv_ref, kr_ref, krrot_ref, g_ref, wk_ref, wv_ref, cos_ref, sin_ref,
                 k_ref, v_ref):
    x = ckv_ref[...].astype(F32)
    ms = jnp.mean(x * x, axis=-1, keepdims=True)
    h = (x * lax.rsqrt(ms + EPS) * g_ref[...]).astype(BF16)
    ak = jnp.dot(h, wk_ref[...], preferred_element_type=F32)
    v_ref[...] = jnp.dot(h, wv_ref[...], preferred_element_type=F32).astype(v_ref.dtype)
    rope = kr_ref[...].astype(F32) * cos_ref[...] + krrot_ref[...].astype(F32) * sin_ref[...]
    for hh in range(MLA_HEADS):
        sl = slice(hh * LANES, (hh + 1) * LANES)
        k_ref[:, sl] = (ak[:, sl] + rope).astype(k_ref.dtype)


def _mla_kv_proj(z, gain, wk, wv, cos_t, sin_t, t_blocks, bm):
    R = z.shape[0]
    N = wk.shape[1]
    return pl.pallas_call(
        _mla_kv_body,
        grid=(R // bm,),
        in_specs=[pl.BlockSpec((bm, MLA_KV_RANK), lambda i: (i, COL_CKV // MLA_KV_RANK)),
                  pl.BlockSpec((bm, LANES), lambda i: (i, COL_KR // LANES)),
                  pl.BlockSpec((bm, LANES), lambda i: (i, COL_KRROT // LANES)),
                  pl.BlockSpec((1, MLA_KV_RANK), lambda i: (0, 0)),
                  pl.BlockSpec((MLA_KV_RANK, N), lambda i: (0, 0)),
                  pl.BlockSpec((MLA_KV_RANK, N), lambda i: (0, 0)),
                  pl.BlockSpec((bm, LANES), lambda i: (i % t_blocks, 0)),
                  pl.BlockSpec((bm, LANES), lambda i: (i % t_blocks, 0))],
        out_specs=[pl.BlockSpec((bm, N), lambda i: (i, 0)),
                   pl.BlockSpec((bm, N), lambda i: (i, 0))],
        out_shape=[jax.ShapeDtypeStruct((R, N), BF16),
                   jax.ShapeDtypeStruct((R, N), BF16)],
        compiler_params=_params(("parallel",)),
        name="mla_kv_proj",
    )(z, z, z, gain.reshape(1, -1), wk, wv, cos_t, sin_t)


def _mla_attn_body(q_ref, k_ref, v_ref, o_ref, *, bq, heads):
    i = pl.program_id(2)
    q_pos = i * bq + lax.broadcasted_iota(jnp.int32, (bq, 1), 0)
    k_iota = lax.broadcasted_iota(jnp.int32, (1, bq), 1)
    for hh in range(heads):
        sl = slice(hh * LANES, (hh + 1) * LANES)
        q = q_ref[:, sl]

        def step(j, carry, q=q, sl=sl):
            m, l, acc = carry
            start = pl.multiple_of(j * bq, bq)
            k = k_ref[pl.ds(start, bq), sl]
            v = v_ref[pl.ds(start, bq), sl]
            s = lax.dot_general(q, k, (((1,), (1,)), ((), ())), preferred_element_type=F32)
            k_pos = start + k_iota
            s = jnp.where((k_pos <= q_pos) & (k_pos >= FRAME_PAD), s, NEG)
            m_new = jnp.maximum(m, jnp.max(s, axis=-1, keepdims=True))
            alpha = jnp.exp(m - m_new)
            p = jnp.exp(s - m_new)
            l = alpha * l + jnp.sum(p, axis=-1, keepdims=True)
            acc = alpha * acc + jnp.dot(p.astype(BF16), v, preferred_element_type=F32)
            return m_new, l, acc

        init = (jnp.full((bq, 1), NEG, F32), jnp.zeros((bq, 1), F32), jnp.zeros((bq, LANES), F32))
        m, l, acc = lax.fori_loop(0, i + 1, step, init)
        o = jnp.where(q_pos >= FRAME_PAD, acc / l, 0.0)
        o_ref[:, hh * MLA_V_DIM:(hh + 1) * MLA_V_DIM] = o[:, :MLA_V_DIM].astype(o_ref.dtype)


def _mla_attention(q, k, v, batch, T, bq):
    R = q.shape[0]
    heads = 2
    groups = MLA_HEADS // heads
    t_blocks = T // bq
    wq = heads * LANES
    return pl.pallas_call(
        functools.partial(_mla_attn_body, bq=bq, heads=heads),
        grid=(batch, groups, t_blocks),
        in_specs=[pl.BlockSpec((bq, wq), lambda b, g, i: (b * t_blocks + i, g)),
                  pl.BlockSpec((T, wq), lambda b, g, i: (b, g)),
                  pl.BlockSpec((T, wq), lambda b, g, i: (b, g))],
        out_specs=pl.BlockSpec((bq, heads * MLA_V_DIM), lambda b, g, i: (b * t_blocks + i, g)),
        out_shape=jax.ShapeDtypeStruct((R, MLA_HEADS * MLA_V_DIM), BF16),
        compiler_params=_params(("parallel", "parallel", "arbitrary")),
        name="mla_attention",
    )(q, k, v)


def _rec_in_body(x_ref, wq_ref, wf_ref, wi_ref, wg_ref, la_ref, lc_ref,
                 q_ref, lf_ref, i_ref, g_ref):
    x = x_ref[...]
    q = jnp.dot(x, wq_ref[...], preferred_element_type=F32)
    q_ref[...] = _silu(q).astype(q_ref.dtype)
    f = jnp.dot(x, wf_ref[...], preferred_element_type=F32)
    log_sig = jnp.minimum(f, 0.0) - jnp.log1p(jnp.exp(-jnp.abs(f)))
    y = lc_ref[...] + log_sig
    la = la_ref[...]
    m = jnp.maximum(la, y)
    lf_ref[...] = m + jnp.log(jnp.exp(la - m) + jnp.exp(y - m))
    i_ref[...] = jnp.dot(x, wi_ref[...], preferred_element_type=F32).astype(i_ref.dtype)
    g = jnp.dot(x, wg_ref[...], preferred_element_type=F32)
    g_ref[...] = _silu(g).astype(g_ref.dtype)


def _rec_in_proj(h, w_in, log_lb, log_1m_lb):
    R, K = h.shape
    D = w_in.shape[1] // 4
    bm = _pick(R, (768, 512, 384, 256, 128))
    bn = _pick(D, (512, 256, 128))
    nb = D // bn

    def wspec(sec):
        return pl.BlockSpec((K, bn), lambda i, j: (0, sec * nb + j))

    ospec = pl.BlockSpec((bm, bn), lambda i, j: (i, j))
    vspec = pl.BlockSpec((1, bn), lambda i, j: (0, j))
    return pl.pallas_call(
        _rec_in_body,
        grid=(R // bm, nb),
        in_specs=[pl.BlockSpec((bm, K), lambda i, j: (i, 0)),
                  wspec(0), wspec(1), wspec(2), wspec(3), vspec, vspec],
        out_specs=[ospec, ospec, ospec, ospec],
        out_shape=[jax.ShapeDtypeStruct((R, D), BF16),
                   jax.ShapeDtypeStruct((R, D), F32),
                   jax.ShapeDtypeStruct((R, D), BF16),
                   jax.ShapeDtypeStruct((R, D), BF16)],
        compiler_params=_params(("parallel", "arbitrary")),
        name="rec_in_proj",
    )(h, w_in, w_in, w_in, w_in, log_lb.reshape(1, D), log_1m_lb.reshape(1, D))


def _rec_body(q_ref, lf_ref, v_ref, gs_ref, gain_ref, o_ref,
              state_ref, b_ref, qf_ref, kf_ref, diag_ref):
    c = pl.program_id(1)
    C = REC_CHUNK
    SUB = REC_SUB
    n_sub = C // SUB
    D = REC_DIM
    contract_last = (((1,), (1,)), ((), ()))
    contract_first = (((0,), (0,)), ((), ()))

    @pl.when(c == 0)
    def _():
        state_ref[...] = jnp.zeros_like(state_ref)
        diag_ref[...] = jnp.zeros_like(diag_ref)

    row = c * C + lax.broadcasted_iota(jnp.int32, (C, 1), 0)
    g = jnp.where(row >= FRAME_PAD, lf_ref[...], 0.0)
    tri = (lax.broadcasted_iota(jnp.int32, (C, C), 0)
           >= lax.broadcasted_iota(jnp.int32, (C, C), 1)).astype(F32)
    b = jnp.dot(tri, g, precision=lax.Precision.HIGHEST, preferred_element_type=F32)
    kk = 1.0 - jnp.exp(g)
    q = q_ref[...].astype(F32)
    b_ref[...] = b
    kf_ref[...] = kk
    qf_ref[...] = q

    b_last = b[C - 1:C, :]
    q_inter = (q * jnp.exp(b)).astype(BF16)
    k_state = (kk * jnp.exp(b_last - b)).astype(BF16)
    state_decay = jnp.exp(b_last)

    q_off, k_off = [], []
    for J in range(n_sub - 1):
        r = b[(J + 1) * SUB - 1:(J + 1) * SUB, :]
        q_off.append((q[(J + 1) * SUB:, :] * jnp.exp(b[(J + 1) * SUB:, :] - r)).astype(BF16))
        k_off.append((kk[J * SUB:(J + 1) * SUB, :]
                      * jnp.exp(r - b[J * SUB:(J + 1) * SUB, :])).astype(BF16))

    t_idx = lax.broadcasted_iota(jnp.int32, (SUB, 1), 0)
    lane = lax.broadcasted_iota(jnp.int32, (SUB, D), 1)
    for I in range(n_sub):
        r0 = I * SUB
        q_i = qf_ref[r0:r0 + SUB, :]
        b_i = b_ref[r0:r0 + SUB, :]

        def key_row(s, carry, I=I, r0=r0, q_i=q_i, b_i=b_i):
            b_s = b_ref[pl.ds(r0 + s, 1), :]
            k_s = kf_ref[pl.ds(r0 + s, 1), :]
            w = jnp.exp(jnp.minimum(b_i - b_s, 0.0))
            x = jnp.where(t_idx >= s, q_i * k_s * w, 0.0)
            for h in range(REC_HEADS):
                col = jnp.sum(x[:, h * D:(h + 1) * D], axis=-1, keepdims=True)
                diag_ref[I, h] = jnp.where(lane == s, col, diag_ref[I, h])
            return carry

        lax.fori_loop(0, SUB, key_row, 0)

    v = v_ref[...]
    gain = gain_ref[...]
    for h in range(REC_HEADS):
        sl = slice(h * D, (h + 1) * D)
        st = state_ref[h]
        o = lax.dot_general(q_inter[:, sl], st.astype(BF16), contract_last,
                            preferred_element_type=F32)
        parts = [o[I * SUB:(I + 1) * SUB, :] for I in range(n_sub)]
        for J in range(n_sub - 1):
            p = lax.dot_general(q_off[J][:, sl], k_off[J][:, sl], contract_last,
                                preferred_element_type=F32)
            oj = jnp.dot(p.astype(BF16), v[J * SUB:(J + 1) * SUB, sl],
                         preferred_element_type=F32)
            for I in range(J + 1, n_sub):
                parts[I] = parts[I] + oj[(I - J - 1) * SUB:(I - J) * SUB, :]
        for I in range(n_sub):
            att = diag_ref[I, h][:, :SUB].astype(BF16)
            parts[I] = parts[I] + jnp.dot(att, v[I * SUB:(I + 1) * SUB, sl],
                                          preferred_element_type=F32)
        o = jnp.concatenate(parts, axis=0)
        upd = lax.dot_general(v[:, sl], k_state[:, sl], contract_first,
                              preferred_element_type=F32)
        state_ref[h] = state_decay[:, sl] * st + upd
        ms = jnp.mean(o * o, axis=-1, keepdims=True)
        y = o * lax.rsqrt(ms + EPS) * gain * gs_ref[:, sl].astype(F32)
        o_ref[:, sl] = y.astype(o_ref.dtype)


def _rec_mixer(qs, log_f, v, gs, out_gain, batch, T):
    R, W = qs.shape
    C = REC_CHUNK
    n_chunks = T // C
    spec = pl.BlockSpec((C, W), lambda b, c: (b * n_chunks + c, 0))
    return pl.pallas_call(
        _rec_body,
        grid=(batch, n_chunks),
        in_specs=[spec, spec, spec, spec, pl.BlockSpec((1, REC_DIM), lambda b, c: (0, 0))],
        out_specs=spec,
        out_shape=jax.ShapeDtypeStruct((R, W), BF16),
        scratch_shapes=[pltpu.VMEM((REC_HEADS, REC_DIM, REC_DIM), F32),
                        pltpu.VMEM((C, W), F32),
                        pltpu.VMEM((C, W), F32),
                        pltpu.VMEM((C, W), F32),
                        pltpu.VMEM((C // REC_SUB, REC_HEADS, REC_SUB, REC_DIM), F32)],
        compiler_params=_params(("parallel", "arbitrary")),
        name="rec_mixer",
    )(qs, log_f, v, gs, out_gain.reshape(1, REC_DIM))


def _att_in_weight(w_in):
    d = w_in.shape[0]
    half = MLA_ROPE_DIM // 2
    w_kr = w_in[:, COL_KR:COL_KR + MLA_ROPE_DIM]
    rot = jnp.concatenate([-w_kr[:, half:], w_kr[:, :half]], axis=1)
    left = jnp.zeros((d, MLA_NOPE_DIM), w_in.dtype)
    right = jnp.zeros((d, LANES - MLA_NOPE_DIM - MLA_ROPE_DIM), w_in.dtype)
    return jnp.concatenate([w_in[:, :COL_KR], left, w_kr, right, left, rot, right], axis=1)


def _mla_q_weights(w_uq):
    r = w_uq.shape[0]
    half = MLA_ROPE_DIM // 2
    w = w_uq.reshape(r, MLA_HEADS, MLA_NOPE_DIM + MLA_ROPE_DIM)
    nope, rope = w[..., :MLA_NOPE_DIM], w[..., MLA_NOPE_DIM:]
    rot = jnp.concatenate([-rope[..., half:], rope[..., :half]], axis=-1)
    fill = jnp.zeros((r, MLA_HEADS, LANES - MLA_NOPE_DIM - MLA_ROPE_DIM), w.dtype)
    plain = jnp.concatenate([nope, rope, fill], axis=-1).reshape(r, MLA_HEADS * LANES)
    rotated = jnp.concatenate([jnp.zeros_like(nope), rot, fill], axis=-1).reshape(r, MLA_HEADS * LANES)
    return plain, rotated


def _mla_kv_weights(w_ukv):
    r = w_ukv.shape[0]
    w = w_ukv.reshape(r, MLA_HEADS, MLA_NOPE_DIM + MLA_V_DIM)
    k_nope, v = w[..., :MLA_NOPE_DIM], w[..., MLA_NOPE_DIM:]
    wk = jnp.concatenate([k_nope, jnp.zeros((r, MLA_HEADS, LANES - MLA_NOPE_DIM), w.dtype)], axis=-1)
    wv = jnp.concatenate([v, jnp.zeros((r, MLA_HEADS, LANES - MLA_V_DIM), w.dtype)], axis=-1)
    return wk.reshape(r, MLA_HEADS * LANES), wv.reshape(r, MLA_HEADS * LANES)


def _rope_tables(T):
    half = MLA_ROPE_DIM // 2
    inv_freq = ROPE_THETA ** (-2.0 * jnp.arange(half, dtype=F32) / MLA_ROPE_DIM)
    pos = jnp.maximum(jnp.arange(T) - FRAME_PAD, 0).astype(F32)
    ang = pos[:, None] * inv_freq[None, :]
    cos, sin = jnp.cos(ang), jnp.sin(ang)
    left = jnp.ones((T, MLA_NOPE_DIM), F32)
    right = jnp.zeros((T, LANES - MLA_NOPE_DIM - MLA_ROPE_DIM), F32)
    cos_t = jnp.concatenate([left, cos, cos, right], axis=1)
    sin_t = jnp.concatenate([0.0 * left, sin, sin, right], axis=1)
    return cos_t, sin_t


def kernel(x, meta_tokens, mix_norm, ffn_norm, final_norm, att_w_in, att_sinks, mla_q_norm,
           mla_w_uq, mla_kv_norm, mla_w_ukv, att_w_out, rec_w_in, rec_lower_bounds, rec_out_norm,
           rec_w_out, ffn_w_up, ffn_w_gate, ffn_conv_w, ffn_conv_b, ffn_w_down):
    B, S, D = x.shape
    depth = mix_norm.shape[0]
    T = FRAME_PAD + N_META + S
    assert T % ATT_BLOCK == 0 and T % REC_CHUNK == 0
    R = B * T

    meta = jnp.broadcast_to(meta_tokens[None].astype(x.dtype), (B, N_META, D))
    hs = jnp.concatenate([jnp.zeros((B, FRAME_PAD, D), x.dtype), meta, x], axis=1).reshape(R, D)

    cos_t, sin_t = _rope_tables(T)
    q_scale = (MLA_NOPE_DIM + MLA_ROPE_DIM) ** -0.5
    sm = jax.nn.softmax(rec_lower_bounds.astype(F32), axis=0)
    lower = jnp.cumsum(sm.at[0].set(0.0), axis=0)
    log_lb, log_1m_lb = jnp.log(lower), jnp.log1p(-lower)

    bq = _pick(T, (384, 128))
    t_blocks = T // bq

    h = _rmsnorm(hs, mix_norm[0], BF16)
    for layer in range(depth):
        idx = layer // 2
        if layer % 2 == 0:
            z = _matmul(h, _att_in_weight(att_w_in[idx]).astype(BF16), BF16)
            o_a = _swa_attention(z, att_sinks[idx].astype(F32), B, T // ATT_BLOCK)
            wq, wq_rot = _mla_q_weights(mla_w_uq[idx])
            wk, wv = _mla_kv_weights(mla_w_ukv[idx])
            q = _mla_q_proj(z, mla_q_norm[idx], wq.astype(BF16), wq_rot.astype(BF16),
                            cos_t * q_scale, sin_t * q_scale, t_blocks, bq)
            k, v = _mla_kv_proj(z, mla_kv_norm[idx], wk.astype(BF16), wv.astype(BF16),
                                cos_t, sin_t, t_blocks, bq)
            o_b = _mla_attention(q, k, v, B, T, bq)
            mixed = jnp.concatenate([o_a, o_b], axis=1)
            w_out = att_w_out[idx]
        else:
            qs, log_f, iv, gs = _rec_in_proj(h, rec_w_in[idx].astype(BF16), log_lb[idx], log_1m_lb[idx])
            mixed = _rec_mixer(qs, log_f, iv, gs, rec_out_norm[idx], B, T)
            w_out = rec_w_out[idx]
        hs, h = _proj_residual_norm(mixed, w_out.astype(BF16), hs, ffn_norm[layer], BF16)
        g = _ffn_up(h, ffn_w_up[layer].astype(BF16), ffn_w_gate[layer].astype(BF16),
                    ffn_conv_w[layer], ffn_conv_b[layer])
        w_down = ffn_w_down[layer].astype(BF16)
        if layer + 1 < depth:
            hs, h = _proj_residual_norm(g, w_down, hs, mix_norm[layer + 1], BF16)
        else:
            (out,) = _proj_residual_norm(g, w_down, hs, final_norm, x.dtype, emit_residual=False)
    return out.reshape(B, T, D)[:, FRAME_PAD + N_META:]
```

```python
import functools

import jax
import jax.numpy as jnp
from jax import lax
from jax.experimental import pallas as pl
from jax.experimental.pallas import tpu as pltpu

F32 = jnp.float32
BF16 = jnp.bfloat16

N_META = 16
EPS = 1e-6
NEG = -1e30
LOG2_E = 1.4426950408889634
ATT_BLOCK = 128
SWA_HEADS = 16
SWA_KV_HEADS = 4
SWA_HEAD_DIM = 64
MLA_HEADS = 16
MLA_Q_RANK = 512
MLA_KV_RANK = 256
MLA_NOPE_DIM = 64
MLA_ROPE_DIM = 32
MLA_V_DIM = 64
ROPE_THETA = 10000.0
REC_HEADS = 16
REC_DIM = 128
CONV_WIDTH = 3

LANES = 128
F32_ROWS = 8
BF16_ROWS = 16
FRAME_PAD = (-N_META) % ATT_BLOCK
REC_CHUNK = 128
VMEM_LIMIT = 56 * 1024 * 1024

SWA_Q = SWA_HEADS * SWA_HEAD_DIM
SWA_KV = SWA_KV_HEADS * SWA_HEAD_DIM
COL_QA = 0
COL_KA = SWA_Q
COL_VA = SWA_Q + SWA_KV
COL_CQ = SWA_Q + 2 * SWA_KV
COL_CKV = COL_CQ + MLA_Q_RANK
COL_KR = COL_CKV + MLA_KV_RANK
COL_KRROT = COL_KR + LANES
ATT_IN_PAD = COL_KRROT + LANES


def _pick(n, candidates):
    for c in candidates:
        if n % c == 0:
            return c
    raise ValueError(f"no block size in {candidates} divides {n}")


def _params(sem):
    return pltpu.CompilerParams(dimension_semantics=sem, vmem_limit_bytes=VMEM_LIMIT)


def _silu(x):
    return x * (1.0 / (1.0 + jnp.exp(-x)))


def _rmsnorm_body(x_ref, g_ref, o_ref):
    x = x_ref[...]
    ms = jnp.mean(x * x, axis=-1, keepdims=True)
    o_ref[...] = (x * lax.rsqrt(ms + EPS) * g_ref[...]).astype(o_ref.dtype)


def _rmsnorm(x, gain, out_dtype):
    R, D = x.shape
    bm = _pick(R, (768, 512, 384, 256, 128))
    return pl.pallas_call(
        _rmsnorm_body,
        grid=(R // bm,),
        in_specs=[pl.BlockSpec((bm, D), lambda i: (i, 0)),
                  pl.BlockSpec((1, D), lambda i: (0, 0))],
        out_specs=pl.BlockSpec((bm, D), lambda i: (i, 0)),
        out_shape=jax.ShapeDtypeStruct((R, D), out_dtype),
        compiler_params=_params(("parallel",)),
        name="rmsnorm",
    )(x, gain.reshape(1, D))


def _matmul_body(x_ref, w_ref, o_ref):
    o_ref[...] = jnp.dot(x_ref[...], w_ref[...], preferred_element_type=F32).astype(o_ref.dtype)


def _matmul(x, w, out_dtype):
    R, K = x.shape
    N = w.shape[1]
    bm = _pick(R, (1408, 768, 512, 384, 256, 128))
    bn = _pick(N, (1280, 1024, 512, 256, 128))
    return pl.pallas_call(
        _matmul_body,
        grid=(R // bm, N // bn),
        in_specs=[pl.BlockSpec((bm, K), lambda i, j: (i, 0)),
                  pl.BlockSpec((K, bn), lambda i, j: (0, j))],
        out_specs=pl.BlockSpec((bm, bn), lambda i, j: (i, j)),
        out_shape=jax.ShapeDtypeStruct((R, N), out_dtype),
        compiler_params=_params(("parallel", "arbitrary")),
        name="att_in_proj",
    )(x, w)


def _proj_res_body(x_ref, w_ref, res_ref, g_ref, *out_refs):
    *hs_refs, h_ref = out_refs
    y = res_ref[...] + jnp.dot(x_ref[...], w_ref[...], preferred_element_type=F32)
    for hs_ref in hs_refs:
        hs_ref[...] = y
    ms = jnp.mean(y * y, axis=-1, keepdims=True)
    h_ref[...] = (y * lax.rsqrt(ms + EPS) * g_ref[...]).astype(h_ref.dtype)


def _proj_residual_norm(x, w, res, gain, h_dtype, emit_residual=True):
    R, K = x.shape
    N = w.shape[1]
    bm = _pick(R, (384, 256, 128))
    row_spec = pl.BlockSpec((bm, N), lambda i: (i, 0))
    out_specs = [row_spec]
    out_shape = [jax.ShapeDtypeStruct((R, N), h_dtype)]
    if emit_residual:
        out_specs = [row_spec] + out_specs
        out_shape = [jax.ShapeDtypeStruct((R, N), F32)] + out_shape
    return pl.pallas_call(
        _proj_res_body,
        grid=(R // bm,),
        in_specs=[pl.BlockSpec((bm, K), lambda i: (i, 0)),
                  pl.BlockSpec((K, N), lambda i: (0, 0), pipeline_mode=pl.Buffered(1)),
                  row_spec,
                  pl.BlockSpec((1, N), lambda i: (0, 0))],
        out_specs=out_specs,
        out_shape=out_shape,
        compiler_params=_params(("parallel",)),
        name="proj_residual_norm",
    )(x, w, res, gain.reshape(1, N))


def _ffn_up_body(x_ref, halo_ref, wu_ref, wg_ref, cw_ref, cb_ref, o_ref, lhs_ref, a_ref):
    bm = x_ref.shape[0]
    halo = halo_ref.shape[0]

    @pl.when(pl.program_id(1) == 0)
    def _():
        lhs_ref[0:halo, :] = halo_ref[...]
        lhs_ref[halo:, :] = x_ref[...]

    a_ref[...] = jnp.dot(lhs_ref[...], wg_ref[...], preferred_element_type=F32)
    u = jnp.dot(x_ref[...], wu_ref[...], preferred_element_type=F32)
    cw = cw_ref[...]
    a = cb_ref[...] + cw[CONV_WIDTH - 1:CONV_WIDTH, :] * a_ref[halo:, :]
    for j in range(CONV_WIDTH - 1):
        back = CONV_WIDTH - 1 - j
        a = a + cw[j:j + 1, :] * a_ref[pl.ds(halo - back, bm), :]
    o_ref[...] = (_silu(a) * u).astype(o_ref.dtype)


def _ffn_up(h, w_up, w_gate, conv_w, conv_b):
    R, K = h.shape
    N = w_up.shape[1]
    bm = _pick(R, (1408, 768, 512, 384, 256, 128))
    bn = _pick(N, (512, 256, 128))
    halo = BF16_ROWS
    hb = bm // halo
    return pl.pallas_call(
        _ffn_up_body,
        grid=(R // bm, N // bn),
        in_specs=[pl.BlockSpec((bm, K), lambda i, j: (i, 0)),
                  pl.BlockSpec((halo, K), lambda i, j: (jnp.maximum(i * hb - 1, 0), 0)),
                  pl.BlockSpec((K, bn), lambda i, j: (0, j)),
                  pl.BlockSpec((K, bn), lambda i, j: (0, j)),
                  pl.BlockSpec((CONV_WIDTH, bn), lambda i, j: (0, j)),
                  pl.BlockSpec((1, bn), lambda i, j: (0, j))],
        out_specs=pl.BlockSpec((bm, bn), lambda i, j: (i, j)),
        out_shape=jax.ShapeDtypeStruct((R, N), BF16),
        scratch_shapes=[pltpu.VMEM((bm + halo, K), BF16),
                        pltpu.VMEM((bm + halo, bn), F32)],
        compiler_params=_params(("parallel", "arbitrary")),
        name="ffn_up",
    )(h, h, w_up, w_gate, conv_w, conv_b.reshape(1, N))


def _swa_body(sink_ref, q_ref, kp_ref, kc_ref, vp_ref, vc_ref, o_ref):
    n = pl.program_id(1)
    blk = q_ref.shape[0]
    dh = SWA_HEAD_DIM
    group = SWA_HEADS // SWA_KV_HEADS
    qi = lax.broadcasted_iota(jnp.int32, (blk, 2 * blk), 0)
    kj = lax.broadcasted_iota(jnp.int32, (blk, 2 * blk), 1)
    rel = qi + blk - kj
    key_pos = n * blk + kj - blk
    mask = (rel >= 0) & (rel < blk) & (key_pos >= FRAME_PAD)
    for g in range(SWA_KV_HEADS):
        ks = slice(g * dh, (g + 1) * dh)
        k = jnp.concatenate([kp_ref[:, ks], kc_ref[:, ks]], axis=0)
        v = jnp.concatenate([vp_ref[:, ks], vc_ref[:, ks]], axis=0)
        for hh in range(group):
            h = g * group + hh
            hs = slice(h * dh, (h + 1) * dh)
            q = q_ref[:, hs] * (dh ** -0.5)
            s = lax.dot_general(q, k, (((1,), (1,)), ((), ())), preferred_element_type=F32)
            s = jnp.where(mask, s, NEG)
            sink = sink_ref[h]
            m = jnp.maximum(jnp.max(s, axis=-1, keepdims=True), sink)
            p = jnp.exp(s - m)
            den = jnp.sum(p, axis=-1, keepdims=True) + jnp.exp(sink - m)
            o = jnp.dot(p.astype(BF16), v, preferred_element_type=F32) / den
            o_ref[:, hs] = o.astype(o_ref.dtype)


def _swa_attention(z, sinks, batch, n_blocks):
    R = z.shape[0]
    blk = ATT_BLOCK
    kcol = COL_KA // SWA_KV
    vcol = COL_VA // SWA_KV

    def cur(b, n):
        return b * n_blocks + n

    def prev(b, n):
        return b * n_blocks + jnp.maximum(n - 1, 0)

    return pl.pallas_call(
        _swa_body,
        grid=(batch, n_blocks),
        in_specs=[pl.BlockSpec(memory_space=pltpu.SMEM),
                  pl.BlockSpec((blk, SWA_Q), lambda b, n: (cur(b, n), 0)),
                  pl.BlockSpec((blk, SWA_KV), lambda b, n: (prev(b, n), kcol)),
                  pl.BlockSpec((blk, SWA_KV), lambda b, n: (cur(b, n), kcol)),
                  pl.BlockSpec((blk, SWA_KV), lambda b, n: (prev(b, n), vcol)),
                  pl.BlockSpec((blk, SWA_KV), lambda b, n: (cur(b, n), vcol))],
        out_specs=pl.BlockSpec((blk, SWA_Q), lambda b, n: (cur(b, n), 0)),
        out_shape=jax.ShapeDtypeStruct((R, SWA_Q), BF16),
        compiler_params=_params(("parallel", "arbitrary")),
        name="swa_attention",
    )(sinks, z, z, z, z, z)


def _mla_q_body(cq_ref, g_ref, w_ref, wrot_ref, cos_ref, sin_ref, bias_ref, o_ref):
    x = cq_ref[...].astype(F32)
    ms = jnp.mean(x * x, axis=-1, keepdims=True)
    h = (x * lax.rsqrt(ms + EPS) * g_ref[...]).astype(BF16)
    a = jnp.dot(h, w_ref[...], preferred_element_type=F32)
    ar = jnp.dot(h, wrot_ref[...], preferred_element_type=F32)
    c = cos_ref[...]
    s = sin_ref[...]
    bias = bias_ref[...]
    for hh in range(MLA_HEADS):
        sl = slice(hh * LANES, (hh + 1) * LANES)
        o_ref[:, sl] = (a[:, sl] * c + ar[:, sl] * s + bias).astype(o_ref.dtype)


def _mla_q_proj(z, gain, w, wrot, cos_t, sin_t, bias_t, t_blocks, bm):
    R = z.shape[0]
    N = w.shape[1]
    table = pl.BlockSpec((bm, LANES), lambda i: (i % t_blocks, 0))
    return pl.pallas_call(
        _mla_q_body,
        grid=(R // bm,),
        in_specs=[pl.BlockSpec((bm, MLA_Q_RANK), lambda i: (i, COL_CQ // MLA_Q_RANK)),
                  pl.BlockSpec((1, MLA_Q_RANK), lambda i: (0, 0)),
                  pl.BlockSpec((MLA_Q_RANK, N), lambda i: (0, 0)),
                  pl.BlockSpec((MLA_Q_RANK, N), lambda i: (0, 0)),
                  table, table, table],
        out_specs=pl.BlockSpec((bm, N), lambda i: (i, 0)),
        out_shape=jax.ShapeDtypeStruct((R, N), BF16),
        compiler_params=_params(("parallel",)),
        name="mla_q_proj",
    )(z, gain.reshape(1, -1), w, wrot, cos_t, sin_t, bias_t)


def _mla_kv_body(ckv_ref, kr_ref, krrot_ref, g_ref, wk_ref, wv_ref, cos_ref, sin_ref, bias_ref,
                 k_ref, v_ref):
    x = ckv_ref[...].astype(F32)
    ms = jnp.mean(x * x, axis=-1, keepdims=True)
    h = (x * lax.rsqrt(ms + EPS) * g_ref[...]).astype(BF16)
    ak = jnp.dot(h, wk_ref[...], preferred_element_type=F32)
    v_ref[...] = jnp.dot(h, wv_ref[...], preferred_element_type=F32).astype(v_ref.dtype)
    rope = (kr_ref[...].astype(F32) * cos_ref[...] + krrot_ref[...].astype(F32) * sin_ref[...]
            + bias_ref[...])
    for hh in range(MLA_HEADS):
        sl = slice(hh * LANES, (hh + 1) * LANES)
        k_ref[:, sl] = (ak[:, sl] + rope).astype(k_ref.dtype)


def _mla_kv_proj(z, gain, wk, wv, cos_t, sin_t, bias_t, t_blocks, bm):
    R = z.shape[0]
    table = pl.BlockSpec((bm, LANES), lambda i: (i % t_blocks, 0))
    return pl.pallas_call(
        _mla_kv_body,
        grid=(R // bm,),
        in_specs=[pl.BlockSpec((bm, MLA_KV_RANK), lambda i: (i, COL_CKV // MLA_KV_RANK)),
                  pl.BlockSpec((bm, LANES), lambda i: (i, COL_KR // LANES)),
                  pl.BlockSpec((bm, LANES), lambda i: (i, COL_KRROT // LANES)),
                  pl.BlockSpec((1, MLA_KV_RANK), lambda i: (0, 0)),
                  pl.BlockSpec(wk.shape, lambda i: (0, 0)),
                  pl.BlockSpec(wv.shape, lambda i: (0, 0)),
                  table, table, table],
        out_specs=[pl.BlockSpec((bm, wk.shape[1]), lambda i: (i, 0)),
                   pl.BlockSpec((bm, wv.shape[1]), lambda i: (i, 0))],
        out_shape=[jax.ShapeDtypeStruct((R, wk.shape[1]), BF16),
                   jax.ShapeDtypeStruct((R, wv.shape[1]), BF16)],
        compiler_params=_params(("parallel",)),
        name="mla_kv_proj",
    )(z, z, z, gain.reshape(1, -1), wk, wv, cos_t, sin_t, bias_t)


def _mla_attn_body(q_ref, k_ref, v_ref, o_ref, vt_ref, *, bq, heads):
    i = pl.program_id(2)
    dv = MLA_V_DIM

    @pl.when(i == 0)
    def _():
        for c in range(vt_ref.shape[0]):
            vt_ref[c] = v_ref[c * bq:(c + 1) * bq, :].astype(F32).T.astype(BF16)

    causal = (lax.broadcasted_iota(jnp.int32, (bq, 1), 0)
              <= lax.broadcasted_iota(jnp.int32, (1, bq), 1))

    def tile(j, carry, masked):
        start = pl.multiple_of(j * bq, bq)
        vt = vt_ref[j]
        out = []
        for hh in range(heads):
            m, l, acc = carry[hh]
            sl = slice(hh * LANES, (hh + 1) * LANES)
            st = lax.dot_general(k_ref[pl.ds(start, bq), sl], q_ref[:, sl],
                                 (((1,), (1,)), ((), ())), preferred_element_type=F32)
            if masked:
                st = jnp.where(causal, st, NEG)
            m_new = jnp.maximum(m, jnp.max(st, axis=0, keepdims=True))
            alpha = jnp.exp2(m - m_new)
            p = jnp.exp2(st - m_new)
            l = alpha * l + jnp.sum(p, axis=0, keepdims=True)
            acc = alpha * acc + jnp.dot(vt[hh * dv:(hh + 1) * dv, :], p.astype(BF16),
                                        preferred_element_type=F32)
            out.append((m_new, l, acc))
        return tuple(out)

    init = tuple((jnp.full((1, bq), NEG, F32), jnp.zeros((1, bq), F32), jnp.zeros((dv, bq), F32))
                 for _ in range(heads))
    carry = lax.fori_loop(0, i, lambda j, c: tile(j, c, False), init)
    carry = tile(i, carry, True)
    out_t = jnp.concatenate([acc / l for _, l, acc in carry], axis=0)
    q_pos = i * bq + lax.broadcasted_iota(jnp.int32, (bq, 1), 0)
    o_ref[...] = jnp.where(q_pos >= FRAME_PAD, out_t.T, 0.0).astype(o_ref.dtype)


def _mla_attention(q, k, v, batch, T, bq):
    R = q.shape[0]
    heads = LANES // MLA_V_DIM
    groups = MLA_HEADS // heads
    t_blocks = T // bq
    wq = heads * LANES
    return pl.pallas_call(
        functools.partial(_mla_attn_body, bq=bq, heads=heads),
        grid=(batch, groups, t_blocks),
        in_specs=[pl.BlockSpec((bq, wq), lambda b, g, i: (b * t_blocks + i, g)),
                  pl.BlockSpec((T, wq), lambda b, g, i: (b, g)),
                  pl.BlockSpec((T, LANES), lambda b, g, i: (b, g))],
        out_specs=pl.BlockSpec((bq, LANES), lambda b, g, i: (b * t_blocks + i, g)),
        out_shape=jax.ShapeDtypeStruct((R, MLA_HEADS * MLA_V_DIM), BF16),
        scratch_shapes=[pltpu.VMEM((t_blocks, LANES, bq), BF16)],
        compiler_params=_params(("parallel", "parallel", "arbitrary")),
        name="mla_attention",
    )(q, k, v)


def _rec_in_body(x_ref, wq_ref, wf_ref, wi_ref, wg_ref, la_ref, lc_ref,
                 q_ref, lf_ref, i_ref, g_ref):
    x = x_ref[...]
    q = jnp.dot(x, wq_ref[...], preferred_element_type=F32)
    q_ref[...] = _silu(q).astype(q_ref.dtype)
    f = jnp.dot(x, wf_ref[...], preferred_element_type=F32)
    log_sig = jnp.minimum(f, 0.0) - jnp.log1p(jnp.exp(-jnp.abs(f)))
    y = lc_ref[...] + log_sig
    la = la_ref[...]
    m = jnp.maximum(la, y)
    lf_ref[...] = m + jnp.log(jnp.exp(la - m) + jnp.exp(y - m))
    i_ref[...] = jnp.dot(x, wi_ref[...], preferred_element_type=F32).astype(i_ref.dtype)
    g = jnp.dot(x, wg_ref[...], preferred_element_type=F32)
    g_ref[...] = _silu(g).astype(g_ref.dtype)


def _rec_in_proj(h, w_in, log_lb, log_1m_lb):
    R, K = h.shape
    D = w_in.shape[1] // 4
    bm = _pick(R, (768, 512, 384, 256, 128))
    bn = _pick(D, (512, 256, 128))
    nb = D // bn

    def wspec(sec):
        return pl.BlockSpec((K, bn), lambda i, j: (0, sec * nb + j))

    ospec = pl.BlockSpec((bm, bn), lambda i, j: (i, j))
    vspec = pl.BlockSpec((1, bn), lambda i, j: (0, j))
    return pl.pallas_call(
        _rec_in_body,
        grid=(R // bm, nb),
        in_specs=[pl.BlockSpec((bm, K), lambda i, j: (i, 0)),
                  wspec(0), wspec(1), wspec(2), wspec(3), vspec, vspec],
        out_specs=[ospec, ospec, ospec, ospec],
        out_shape=[jax.ShapeDtypeStruct((R, D), BF16),
                   jax.ShapeDtypeStruct((R, D), F32),
                   jax.ShapeDtypeStruct((R, D), BF16),
                   jax.ShapeDtypeStruct((R, D), BF16)],
        compiler_params=_params(("parallel", "arbitrary")),
        name="rec_in_proj",
    )(h, w_in, w_in, w_in, w_in, log_lb.reshape(1, D), log_1m_lb.reshape(1, D))


def _segment_tail(cum, w):
    C, W = cum.shape
    if 2 * w >= F32_ROWS * 2:
        pieces = [jnp.broadcast_to(cum[r:r + 1, :], (2 * w, W))
                  for r in range(w - 1, C, 2 * w)]
        return pieces[0] if len(pieces) == 1 else jnp.concatenate(pieces, axis=0)
    tiles = cum.reshape(C // F32_ROWS, F32_ROWS, W)
    sub = lax.broadcasted_iota(jnp.int32, (1, F32_ROWS, 1), 1)
    out = None
    for r in reversed(range(w - 1, F32_ROWS, 2 * w)):
        piece = jnp.broadcast_to(tiles[:, r:r + 1, :], tiles.shape)
        out = piece if out is None else jnp.where(sub < r + w + 1, piece, out)
    return out.reshape(C, W)


def _rec_body(q_ref, lf_ref, v_ref, gs_ref, gain_ref, o_ref, state_ref):
    c = pl.program_id(1)
    C = q_ref.shape[0]
    D = REC_DIM
    contract_last = (((1,), (1,)), ((), ()))
    contract_first = (((0,), (0,)), ((), ()))

    @pl.when(c == 0)
    def _():
        state_ref[...] = jnp.zeros_like(state_ref)

    row = lax.broadcasted_iota(jnp.int32, (C, 1), 0)
    g = jnp.where(c * C + row >= FRAME_PAD, lf_ref[...], 0.0)
    kk = 1.0 - jnp.exp(g)
    q16 = q_ref[...]
    q = q16.astype(F32)

    ti = lax.broadcasted_iota(jnp.int32, (C, C), 0)
    si = lax.broadcasted_iota(jnp.int32, (C, C), 1)
    bit_diff = jnp.where(ti > si, ti ^ si, 0)
    cum = g
    levels = []
    w = 1
    while w < C:
        upper = (row & w) != 0
        tail = _segment_tail(cum, w)
        f = jnp.exp(jnp.where(upper, cum, tail - cum))
        pair_mask = (bit_diff >= w) & (bit_diff < 2 * w)
        levels.append(((q * f).astype(BF16), (kk * f).astype(BF16), pair_mask))
        cum = cum + jnp.where(upper, tail, 0.0)
        w *= 2
    b = cum
    same_row = ti == si
    k16 = kk.astype(BF16)

    b_last = b[C - 1:C, :]
    q_inter = (q * jnp.exp(b)).astype(BF16)
    k_state = (kk * jnp.exp(b_last - b)).astype(BF16)
    state_decay = jnp.exp(b_last)

    v = v_ref[...]
    gain = gain_ref[...]
    for h in range(REC_HEADS):
        sl = slice(h * D, (h + 1) * D)
        st = state_ref[h]
        att = jnp.where(same_row,
                        lax.dot_general(q16[:, sl], k16[:, sl], contract_last,
                                        preferred_element_type=F32), 0.0)
        for q_lvl, k_lvl, pair_mask in levels:
            att = jnp.where(pair_mask,
                            lax.dot_general(q_lvl[:, sl], k_lvl[:, sl], contract_last,
                                            preferred_element_type=F32), att)
        o = (lax.dot_general(q_inter[:, sl], st.astype(BF16), contract_last,
                             preferred_element_type=F32)
             + jnp.dot(att.astype(BF16), v[:, sl], preferred_element_type=F32))
        upd = lax.dot_general(v[:, sl], k_state[:, sl], contract_first,
                              preferred_element_type=F32)
        state_ref[h] = state_decay[:, sl] * st + upd
        ms = jnp.mean(o * o, axis=-1, keepdims=True)
        y = o * lax.rsqrt(ms + EPS) * gain * gs_ref[:, sl].astype(F32)
        o_ref[:, sl] = y.astype(o_ref.dtype)


def _rec_mixer(qs, log_f, v, gs, out_gain, batch, T):
    R, W = qs.shape
    C = REC_CHUNK
    n_chunks = T // C
    spec = pl.BlockSpec((C, W), lambda b, c: (b * n_chunks + c, 0))
    return pl.pallas_call(
        _rec_body,
        grid=(batch, n_chunks),
        in_specs=[spec, spec, spec, spec, pl.BlockSpec((1, REC_DIM), lambda b, c: (0, 0))],
        out_specs=spec,
        out_shape=jax.ShapeDtypeStruct((R, W), BF16),
        scratch_shapes=[pltpu.VMEM((REC_HEADS, REC_DIM, REC_DIM), F32)],
        compiler_params=_params(("parallel", "arbitrary")),
        name="rec_mixer",
    )(qs, log_f, v, gs, out_gain.reshape(1, REC_DIM))


def _att_in_weight(w_in):
    d = w_in.shape[0]
    half = MLA_ROPE_DIM // 2
    w_kr = w_in[:, COL_KR:COL_KR + MLA_ROPE_DIM]
    rot = jnp.concatenate([-w_kr[:, half:], w_kr[:, :half]], axis=1)
    left = jnp.zeros((d, MLA_NOPE_DIM), w_in.dtype)
    right = jnp.zeros((d, LANES - MLA_NOPE_DIM - MLA_ROPE_DIM), w_in.dtype)
    return jnp.concatenate([w_in[:, :COL_KR], left, w_kr, right, left, rot, right], axis=1)


def _mla_q_weights(w_uq):
    r = w_uq.shape[0]
    half = MLA_ROPE_DIM // 2
    w = w_uq.reshape(r, MLA_HEADS, MLA_NOPE_DIM + MLA_ROPE_DIM)
    nope, rope = w[..., :MLA_NOPE_DIM], w[..., MLA_NOPE_DIM:]
    rot = jnp.concatenate([-rope[..., half:], rope[..., :half]], axis=-1)
    fill = jnp.zeros((r, MLA_HEADS, LANES - MLA_NOPE_DIM - MLA_ROPE_DIM), w.dtype)
    plain = jnp.concatenate([nope, rope, fill], axis=-1).reshape(r, MLA_HEADS * LANES)
    rotated = jnp.concatenate([jnp.zeros_like(nope), rot, fill], axis=-1).reshape(r, MLA_HEADS * LANES)
    return plain, rotated


def _mla_kv_weights(w_ukv):
    r = w_ukv.shape[0]
    w = w_ukv.reshape(r, MLA_HEADS, MLA_NOPE_DIM + MLA_V_DIM)
    k_nope, v = w[..., :MLA_NOPE_DIM], w[..., MLA_NOPE_DIM:]
    wk = jnp.concatenate([k_nope, jnp.zeros((r, MLA_HEADS, LANES - MLA_NOPE_DIM), w.dtype)], axis=-1)
    return wk.reshape(r, MLA_HEADS * LANES), v.reshape(r, MLA_HEADS * MLA_V_DIM)


def _rope_tables(T):
    half = MLA_ROPE_DIM // 2
    inv_freq = ROPE_THETA ** (-2.0 * jnp.arange(half, dtype=F32) / MLA_ROPE_DIM)
    pos = jnp.maximum(jnp.arange(T) - FRAME_PAD, 0).astype(F32)
    ang = pos[:, None] * inv_freq[None, :]
    cos, sin = jnp.cos(ang), jnp.sin(ang)
    left = jnp.ones((T, MLA_NOPE_DIM), F32)
    right = jnp.zeros((T, LANES - MLA_NOPE_DIM - MLA_ROPE_DIM), F32)
    cos_t = jnp.concatenate([left, cos, cos, right], axis=1)
    sin_t = jnp.concatenate([0.0 * left, sin, sin, right], axis=1)
    spare = (jnp.arange(LANES) == MLA_NOPE_DIM + MLA_ROPE_DIM).astype(F32)[None, :]
    is_pad = (jnp.arange(T) < FRAME_PAD).astype(F32)[:, None]
    q_bias = jnp.broadcast_to(spare, (T, LANES))
    k_bias = NEG * is_pad * spare
    return cos_t, sin_t, q_bias, k_bias


def kernel(x, meta_tokens, mix_norm, ffn_norm, final_norm, att_w_in, att_sinks, mla_q_norm,
           mla_w_uq, mla_kv_norm, mla_w_ukv, att_w_out, rec_w_in, rec_lower_bounds, rec_out_norm,
           rec_w_out, ffn_w_up, ffn_w_gate, ffn_conv_w, ffn_conv_b, ffn_w_down):
    B, S, D = x.shape
    depth = mix_norm.shape[0]
    T = FRAME_PAD + N_META + S
    assert T % ATT_BLOCK == 0 and T % REC_CHUNK == 0
    R = B * T

    meta = jnp.broadcast_to(meta_tokens[None].astype(x.dtype), (B, N_META, D))
    hs = jnp.concatenate([jnp.zeros((B, FRAME_PAD, D), x.dtype), meta, x], axis=1).reshape(R, D)

    cos_t, sin_t, q_bias, k_bias = _rope_tables(T)
    q_scale = (MLA_NOPE_DIM + MLA_ROPE_DIM) ** -0.5 * LOG2_E
    sm = jax.nn.softmax(rec_lower_bounds.astype(F32), axis=0)
    lower = jnp.cumsum(sm.at[0].set(0.0), axis=0)
    log_lb, log_1m_lb = jnp.log(lower), jnp.log1p(-lower)

    bq = _pick(T, (384, 128))
    t_blocks = T // bq

    h = _rmsnorm(hs, mix_norm[0], BF16)
    for layer in range(depth):
        idx = layer // 2
        if layer % 2 == 0:
            z = _matmul(h, _att_in_weight(att_w_in[idx]).astype(BF16), BF16)
            o_a = _swa_attention(z, att_sinks[idx].astype(F32), B, T // ATT_BLOCK)
            wq, wq_rot = _mla_q_weights(mla_w_uq[idx])
            wk, wv = _mla_kv_weights(mla_w_ukv[idx])
            q = _mla_q_proj(z, mla_q_norm[idx], wq.astype(BF16), wq_rot.astype(BF16),
                            cos_t * q_scale, sin_t * q_scale, q_bias, t_blocks, bq)
            k, v = _mla_kv_proj(z, mla_kv_norm[idx], wk.astype(BF16), wv.astype(BF16),
                                cos_t, sin_t, k_bias, t_blocks, bq)
            o_b = _mla_attention(q, k, v, B, T, bq)
            mixed = jnp.concatenate([o_a, o_b], axis=1)
            w_out = att_w_out[idx]
        else:
            qs, log_f, iv, gs = _rec_in_proj(h, rec_w_in[idx].astype(BF16), log_lb[idx], log_1m_lb[idx])
            mixed = _rec_mixer(qs, log_f, iv, gs, rec_out_norm[idx], B, T)
            w_out = rec_w_out[idx]
        hs, h = _proj_residual_norm(mixed, w_out.astype(BF16), hs, ffn_norm[layer], BF16)
        g = _ffn_up(h, ffn_w_up[layer].astype(BF16), ffn_w_gate[layer].astype(BF16),
                    ffn_conv_w[layer], ffn_conv_b[layer])
        w_down = ffn_w_down[layer].astype(BF16)
        if layer + 1 < depth:
            hs, h = _proj_residual_norm(g, w_down, hs, mix_norm[layer + 1], BF16)
        else:
            (out,) = _proj_residual_norm(g, w_down, hs, final_norm, x.dtype, emit_residual=False)
    return out.reshape(B, T, D)[:, FRAME_PAD + N_META:]
```

```python
import functools

import jax
import jax.numpy as jnp
from jax import lax
from jax.experimental import pallas as pl
from jax.experimental.pallas import tpu as pltpu

F32 = jnp.float32
BF16 = jnp.bfloat16

N_META = 16
EPS = 1e-6
NEG = -1e30
LOG2_E = 1.4426950408889634
ATT_BLOCK = 128
SWA_HEADS = 16
SWA_KV_HEADS = 4
SWA_HEAD_DIM = 64
MLA_HEADS = 16
MLA_Q_RANK = 512
MLA_KV_RANK = 256
MLA_NOPE_DIM = 64
MLA_ROPE_DIM = 32
MLA_V_DIM = 64
ROPE_THETA = 10000.0
REC_HEADS = 16
REC_DIM = 128
CONV_WIDTH = 3

LANES = 128
MXU_COLS = 256
F32_ROWS = 8
BF16_ROWS = 16
FRAME_PAD = (-N_META) % ATT_BLOCK
REC_CHUNK = 128
VMEM_LIMIT = 56 * 1024 * 1024

SWA_Q = SWA_HEADS * SWA_HEAD_DIM
SWA_KV = SWA_KV_HEADS * SWA_HEAD_DIM
COL_QA = 0
COL_KA = SWA_Q
COL_VA = SWA_Q + SWA_KV
COL_CQ = SWA_Q + 2 * SWA_KV
COL_CKV = COL_CQ + MLA_Q_RANK
COL_KR = COL_CKV + MLA_KV_RANK
COL_KRROT = COL_KR + LANES
ATT_IN_PAD = COL_KRROT + LANES


def _pick(n, candidates):
    for c in candidates:
        if n % c == 0:
            return c
    raise ValueError(f"no block size in {candidates} divides {n}")


def _params(sem):
    return pltpu.CompilerParams(dimension_semantics=sem, vmem_limit_bytes=VMEM_LIMIT)


def _silu(x):
    return x * (1.0 / (1.0 + jnp.exp(-x)))


def _rmsnorm_body(x_ref, g_ref, o_ref):
    x = x_ref[...]
    ms = jnp.mean(x * x, axis=-1, keepdims=True)
    o_ref[...] = (x * lax.rsqrt(ms + EPS) * g_ref[...]).astype(o_ref.dtype)


def _rmsnorm(x, gain, out_dtype):
    R, D = x.shape
    bm = _pick(R, (768, 512, 384, 256, 128))
    return pl.pallas_call(
        _rmsnorm_body,
        grid=(R // bm,),
        in_specs=[pl.BlockSpec((bm, D), lambda i: (i, 0)),
                  pl.BlockSpec((1, D), lambda i: (0, 0))],
        out_specs=pl.BlockSpec((bm, D), lambda i: (i, 0)),
        out_shape=jax.ShapeDtypeStruct((R, D), out_dtype),
        compiler_params=_params(("parallel",)),
        name="rmsnorm",
    )(x, gain.reshape(1, D))


def _matmul_body(x_ref, w_ref, o_ref):
    o_ref[...] = jnp.dot(x_ref[...], w_ref[...], preferred_element_type=F32).astype(o_ref.dtype)


def _matmul(x, w, out_dtype):
    R, K = x.shape
    N = w.shape[1]
    bm = _pick(R, (1408, 768, 512, 384, 256, 128))
    bn = _pick(N, (1280, 1024, 512, 256, 128))
    return pl.pallas_call(
        _matmul_body,
        grid=(R // bm, N // bn),
        in_specs=[pl.BlockSpec((bm, K), lambda i, j: (i, 0)),
                  pl.BlockSpec((K, bn), lambda i, j: (0, j))],
        out_specs=pl.BlockSpec((bm, bn), lambda i, j: (i, j)),
        out_shape=jax.ShapeDtypeStruct((R, N), out_dtype),
        compiler_params=_params(("parallel", "arbitrary")),
        name="att_in_proj",
    )(x, w)


def _proj_res_body(x_ref, w_ref, res_ref, g_ref, *out_refs):
    *hs_refs, h_ref = out_refs
    y = res_ref[...] + jnp.dot(x_ref[...], w_ref[...], preferred_element_type=F32)
    for hs_ref in hs_refs:
        hs_ref[...] = y
    ms = jnp.mean(y * y, axis=-1, keepdims=True)
    h_ref[...] = (y * lax.rsqrt(ms + EPS) * g_ref[...]).astype(h_ref.dtype)


def _proj_residual_norm(x, w, res, gain, h_dtype, emit_residual=True):
    R, K = x.shape
    N = w.shape[1]
    bm = _pick(R, (384, 256, 128))
    row_spec = pl.BlockSpec((bm, N), lambda i: (i, 0))
    out_specs = [row_spec]
    out_shape = [jax.ShapeDtypeStruct((R, N), h_dtype)]
    if emit_residual:
        out_specs = [row_spec] + out_specs
        out_shape = [jax.ShapeDtypeStruct((R, N), F32)] + out_shape
    return pl.pallas_call(
        _proj_res_body,
        grid=(R // bm,),
        in_specs=[pl.BlockSpec((bm, K), lambda i: (i, 0)),
                  pl.BlockSpec((K, N), lambda i: (0, 0), pipeline_mode=pl.Buffered(1)),
                  row_spec,
                  pl.BlockSpec((1, N), lambda i: (0, 0))],
        out_specs=out_specs,
        out_shape=out_shape,
        compiler_params=_params(("parallel",)),
        name="proj_residual_norm",
    )(x, w, res, gain.reshape(1, N))


def _ffn_up_body(x_ref, halo_ref, wu_ref, wg_ref, cw_ref, cb_ref, o_ref, lhs_ref, a_ref):
    bm = x_ref.shape[0]
    halo = halo_ref.shape[0]

    @pl.when(pl.program_id(1) == 0)
    def _():
        lhs_ref[0:halo, :] = halo_ref[...]
        lhs_ref[halo:, :] = x_ref[...]

    a_ref[...] = jnp.dot(lhs_ref[...], wg_ref[...], preferred_element_type=F32)
    u = jnp.dot(x_ref[...], wu_ref[...], preferred_element_type=F32)
    cw = cw_ref[...]
    a = cb_ref[...] + cw[CONV_WIDTH - 1:CONV_WIDTH, :] * a_ref[halo:, :]
    for j in range(CONV_WIDTH - 1):
        back = CONV_WIDTH - 1 - j
        a = a + cw[j:j + 1, :] * a_ref[pl.ds(halo - back, bm), :]
    o_ref[...] = (_silu(a) * u).astype(o_ref.dtype)


def _ffn_up(h, w_up, w_gate, conv_w, conv_b):
    R, K = h.shape
    N = w_up.shape[1]
    bm = _pick(R, (1408, 768, 512, 384, 256, 128))
    bn = _pick(N, (512, 256, 128))
    halo = BF16_ROWS
    hb = bm // halo
    return pl.pallas_call(
        _ffn_up_body,
        grid=(R // bm, N // bn),
        in_specs=[pl.BlockSpec((bm, K), lambda i, j: (i, 0)),
                  pl.BlockSpec((halo, K), lambda i, j: (jnp.maximum(i * hb - 1, 0), 0)),
                  pl.BlockSpec((K, bn), lambda i, j: (0, j)),
                  pl.BlockSpec((K, bn), lambda i, j: (0, j)),
                  pl.BlockSpec((CONV_WIDTH, bn), lambda i, j: (0, j)),
                  pl.BlockSpec((1, bn), lambda i, j: (0, j))],
        out_specs=pl.BlockSpec((bm, bn), lambda i, j: (i, j)),
        out_shape=jax.ShapeDtypeStruct((R, N), BF16),
        scratch_shapes=[pltpu.VMEM((bm + halo, K), BF16),
                        pltpu.VMEM((bm + halo, bn), F32)],
        compiler_params=_params(("parallel", "arbitrary")),
        name="ffn_up",
    )(h, h, w_up, w_gate, conv_w, conv_b.reshape(1, N))


def _swa_body(sink_ref, q_ref, kp_ref, kc_ref, vp_ref, vc_ref, o_ref):
    n = pl.program_id(1)
    blk = q_ref.shape[0]
    dh = SWA_HEAD_DIM
    group = SWA_HEADS // SWA_KV_HEADS
    qi = lax.broadcasted_iota(jnp.int32, (blk, 2 * blk), 0)
    kj = lax.broadcasted_iota(jnp.int32, (blk, 2 * blk), 1)
    rel = qi + blk - kj
    key_pos = n * blk + kj - blk
    mask = (rel >= 0) & (rel < blk) & (key_pos >= FRAME_PAD)
    for g in range(SWA_KV_HEADS):
        ks = slice(g * dh, (g + 1) * dh)
        k = jnp.concatenate([kp_ref[:, ks], kc_ref[:, ks]], axis=0)
        v = jnp.concatenate([vp_ref[:, ks], vc_ref[:, ks]], axis=0)
        for hh in range(group):
            h = g * group + hh
            hs = slice(h * dh, (h + 1) * dh)
            q = q_ref[:, hs] * (dh ** -0.5)
            s = lax.dot_general(q, k, (((1,), (1,)), ((), ())), preferred_element_type=F32)
            s = jnp.where(mask, s, NEG)
            sink = sink_ref[h]
            m = jnp.maximum(jnp.max(s, axis=-1, keepdims=True), sink)
            p = jnp.exp(s - m)
            den = jnp.sum(p, axis=-1, keepdims=True) + jnp.exp(sink - m)
            o = jnp.dot(p.astype(BF16), v, preferred_element_type=F32) / den
            o_ref[:, hs] = o.astype(o_ref.dtype)


def _swa_attention(z, sinks, batch, n_blocks):
    R = z.shape[0]
    blk = ATT_BLOCK
    kcol = COL_KA // SWA_KV
    vcol = COL_VA // SWA_KV

    def cur(b, n):
        return b * n_blocks + n

    def prev(b, n):
        return b * n_blocks + jnp.maximum(n - 1, 0)

    return pl.pallas_call(
        _swa_body,
        grid=(batch, n_blocks),
        in_specs=[pl.BlockSpec(memory_space=pltpu.SMEM),
                  pl.BlockSpec((blk, SWA_Q), lambda b, n: (cur(b, n), 0)),
                  pl.BlockSpec((blk, SWA_KV), lambda b, n: (prev(b, n), kcol)),
                  pl.BlockSpec((blk, SWA_KV), lambda b, n: (cur(b, n), kcol)),
                  pl.BlockSpec((blk, SWA_KV), lambda b, n: (prev(b, n), vcol)),
                  pl.BlockSpec((blk, SWA_KV), lambda b, n: (cur(b, n), vcol))],
        out_specs=pl.BlockSpec((blk, SWA_Q), lambda b, n: (cur(b, n), 0)),
        out_shape=jax.ShapeDtypeStruct((R, SWA_Q), BF16),
        compiler_params=_params(("parallel", "arbitrary")),
        name="swa_attention",
    )(sinks, z, z, z, z, z)


def _mla_q_body(cq_ref, g_ref, w_ref, wrot_ref, cos_ref, sin_ref, bias_ref, o_ref):
    x = cq_ref[...].astype(F32)
    ms = jnp.mean(x * x, axis=-1, keepdims=True)
    h = (x * lax.rsqrt(ms + EPS) * g_ref[...]).astype(BF16)
    a = jnp.dot(h, w_ref[...], preferred_element_type=F32)
    ar = jnp.dot(h, wrot_ref[...], preferred_element_type=F32)
    c = cos_ref[...]
    s = sin_ref[...]
    bias = bias_ref[...]
    for hh in range(MLA_HEADS):
        sl = slice(hh * LANES, (hh + 1) * LANES)
        o_ref[:, sl] = (a[:, sl] * c + ar[:, sl] * s + bias).astype(o_ref.dtype)


def _mla_q_proj(z, gain, w, wrot, cos_t, sin_t, bias_t, t_blocks, bm):
    R = z.shape[0]
    N = w.shape[1]
    table = pl.BlockSpec((bm, LANES), lambda i: (i % t_blocks, 0))
    return pl.pallas_call(
        _mla_q_body,
        grid=(R // bm,),
        in_specs=[pl.BlockSpec((bm, MLA_Q_RANK), lambda i: (i, COL_CQ // MLA_Q_RANK)),
                  pl.BlockSpec((1, MLA_Q_RANK), lambda i: (0, 0)),
                  pl.BlockSpec((MLA_Q_RANK, N), lambda i: (0, 0)),
                  pl.BlockSpec((MLA_Q_RANK, N), lambda i: (0, 0)),
                  table, table, table],
        out_specs=pl.BlockSpec((bm, N), lambda i: (i, 0)),
        out_shape=jax.ShapeDtypeStruct((R, N), BF16),
        compiler_params=_params(("parallel",)),
        name="mla_q_proj",
    )(z, gain.reshape(1, -1), w, wrot, cos_t, sin_t, bias_t)


def _mla_kv_body(ckv_ref, kr_ref, krrot_ref, g_ref, wk_ref, wv_ref, cos_ref, sin_ref, bias_ref,
                 k_ref, v_ref):
    x = ckv_ref[...].astype(F32)
    ms = jnp.mean(x * x, axis=-1, keepdims=True)
    h = (x * lax.rsqrt(ms + EPS) * g_ref[...]).astype(BF16)
    ak = jnp.dot(h, wk_ref[...], preferred_element_type=F32)
    v_ref[...] = jnp.dot(h, wv_ref[...], preferred_element_type=F32).astype(v_ref.dtype)
    rope = (kr_ref[...].astype(F32) * cos_ref[...] + krrot_ref[...].astype(F32) * sin_ref[...]
            + bias_ref[...])
    for hh in range(MLA_HEADS):
        sl = slice(hh * LANES, (hh + 1) * LANES)
        k_ref[:, sl] = (ak[:, sl] + rope).astype(k_ref.dtype)


def _mla_kv_proj(z, gain, wk, wv, cos_t, sin_t, bias_t, t_blocks, bm):
    R = z.shape[0]
    table = pl.BlockSpec((bm, LANES), lambda i: (i % t_blocks, 0))
    return pl.pallas_call(
        _mla_kv_body,
        grid=(R // bm,),
        in_specs=[pl.BlockSpec((bm, MLA_KV_RANK), lambda i: (i, COL_CKV // MLA_KV_RANK)),
                  pl.BlockSpec((bm, LANES), lambda i: (i, COL_KR // LANES)),
                  pl.BlockSpec((bm, LANES), lambda i: (i, COL_KRROT // LANES)),
                  pl.BlockSpec((1, MLA_KV_RANK), lambda i: (0, 0)),
                  pl.BlockSpec(wk.shape, lambda i: (0, 0)),
                  pl.BlockSpec(wv.shape, lambda i: (0, 0)),
                  table, table, table],
        out_specs=[pl.BlockSpec((bm, wk.shape[1]), lambda i: (i, 0)),
                   pl.BlockSpec((bm, wv.shape[1]), lambda i: (i, 0))],
        out_shape=[jax.ShapeDtypeStruct((R, wk.shape[1]), BF16),
                   jax.ShapeDtypeStruct((R, wv.shape[1]), BF16)],
        compiler_params=_params(("parallel",)),
        name="mla_kv_proj",
    )(z, z, z, gain.reshape(1, -1), wk, wv, cos_t, sin_t, bias_t)


def _lane_blocks(s):
    return [s[:, c * LANES:(c + 1) * LANES] for c in range(s.shape[1] // LANES)]


def _mla_attn_body(q_ref, k_ref, v_ref, o_ref, vext_ref, *, bq, heads):
    i = pl.program_id(2)
    dv = MLA_V_DIM
    kb = MXU_COLS
    T = k_ref.shape[0]
    lane = lax.broadcasted_iota(jnp.int32, (1, LANES), 1)
    own = [(lane >= hh * dv) & (lane < (hh + 1) * dv) for hh in range(heads)]

    @pl.when(i == 0)
    def _():
        for c in range(T // bq):
            rows = slice(c * bq, (c + 1) * bq)
            vv = v_ref[rows, :]
            for hh in range(heads):
                vext_ref[hh, rows, :] = jnp.where(own[hh], vv, jnp.ones_like(vv))

    def scores(hh, start, size, row0=0):
        sl = slice(hh * LANES, (hh + 1) * LANES)
        return lax.dot_general(q_ref[row0:, sl], k_ref[pl.ds(start, size), sl],
                               (((1,), (1,)), ((), ())), preferred_element_type=F32)

    def online(state, s, vals):
        m, acc = state
        blocks = _lane_blocks(s)
        top = blocks[0]
        for blk in blocks[1:]:
            top = jnp.maximum(top, blk)
        m_new = jnp.maximum(m, jnp.max(top, axis=-1, keepdims=True))
        p = jnp.concatenate([jnp.exp2(blk - m_new) for blk in blocks], axis=1).astype(BF16)
        acc = jnp.exp2(m - m_new) * acc + jnp.dot(p, vals, preferred_element_type=F32)
        return m_new, acc

    def values(hh, start, size):
        return vext_ref[hh, pl.ds(start, size), :]

    n_full = (i * bq) // kb
    rest_start = pl.multiple_of(n_full * kb, LANES)
    has_rest = i * bq > n_full * kb
    diag_start = pl.multiple_of(i * bq, LANES)
    def full_tile(j, states):
        start = pl.multiple_of(j * kb, kb)
        return tuple(online(states[hh], scores(hh, start, kb), values(hh, start, kb))
                     for hh in range(heads))

    states = lax.fori_loop(
        0, n_full, full_tile,
        tuple((jnp.full((bq, LANES), NEG, F32), jnp.zeros((bq, LANES), F32)) for _ in range(heads)))
    out = jnp.zeros((bq, LANES), F32)
    for hh in range(heads):
        s_rest = jnp.where(has_rest, scores(hh, rest_start, LANES), NEG)
        m, acc = online(states[hh], s_rest, values(hh, rest_start, LANES))
        for k0 in range(0, bq, kb):
            size = min(kb, bq - k0)
            start = pl.multiple_of(diag_start + k0, LANES)
            s = scores(hh, start, size, row0=k0)
            lower = (lax.broadcasted_iota(jnp.int32, (size, 1), 0)
                     >= lax.broadcasted_iota(jnp.int32, (1, size), 1))
            s_head = jnp.where(lower, s[:size], NEG)
            s = s_head if s.shape[0] == size else jnp.concatenate([s_head, s[size:]], axis=0)
            m_low, acc_low = online((m[k0:], acc[k0:]), s, values(hh, start, size))
            if k0 > 0:
                m_low = jnp.concatenate([m[:k0], m_low], axis=0)
                acc_low = jnp.concatenate([acc[:k0], acc_low], axis=0)
            m, acc = m_low, acc_low
        denom = pltpu.roll(acc, dv, axis=1)
        out = jnp.where(own[hh], acc / denom, out)
    q_pos = i * bq + lax.broadcasted_iota(jnp.int32, (bq, 1), 0)
    o_ref[...] = jnp.where(q_pos >= FRAME_PAD, out, 0.0).astype(o_ref.dtype)


def _mla_attention(q, k, v, batch, T, bq):
    R = q.shape[0]
    heads = LANES // MLA_V_DIM
    groups = MLA_HEADS // heads
    t_blocks = T // bq
    wq = heads * LANES
    return pl.pallas_call(
        functools.partial(_mla_attn_body, bq=bq, heads=heads),
        grid=(batch, groups, t_blocks),
        in_specs=[pl.BlockSpec((bq, wq), lambda b, g, i: (b * t_blocks + i, g)),
                  pl.BlockSpec((T, wq), lambda b, g, i: (b, g)),
                  pl.BlockSpec((T, LANES), lambda b, g, i: (b, g))],
        out_specs=pl.BlockSpec((bq, LANES), lambda b, g, i: (b * t_blocks + i, g)),
        out_shape=jax.ShapeDtypeStruct((R, MLA_HEADS * MLA_V_DIM), BF16),
        scratch_shapes=[pltpu.VMEM((heads, T, LANES), BF16)],
        compiler_params=_params(("parallel", "parallel", "arbitrary")),
        name="mla_attention",
    )(q, k, v)


def _rec_in_body(x_ref, wq_ref, wf_ref, wi_ref, wg_ref, la_ref, lc_ref,
                 q_ref, lf_ref, i_ref, g_ref):
    x = x_ref[...]
    q = jnp.dot(x, wq_ref[...], preferred_element_type=F32)
    q_ref[...] = _silu(q).astype(q_ref.dtype)
    f = jnp.dot(x, wf_ref[...], preferred_element_type=F32)
    log_sig = jnp.minimum(f, 0.0) - jnp.log1p(jnp.exp(-jnp.abs(f)))
    y = lc_ref[...] + log_sig
    la = la_ref[...]
    m = jnp.maximum(la, y)
    lf_ref[...] = m + jnp.log(jnp.exp(la - m) + jnp.exp(y - m))
    i_ref[...] = jnp.dot(x, wi_ref[...], preferred_element_type=F32).astype(i_ref.dtype)
    g = jnp.dot(x, wg_ref[...], preferred_element_type=F32)
    g_ref[...] = _silu(g).astype(g_ref.dtype)


def _rec_in_proj(h, w_in, log_lb, log_1m_lb):
    R, K = h.shape
    D = w_in.shape[1] // 4
    bm = _pick(R, (768, 512, 384, 256, 128))
    bn = _pick(D, (512, 256, 128))
    nb = D // bn

    def wspec(sec):
        return pl.BlockSpec((K, bn), lambda i, j: (0, sec * nb + j))

    ospec = pl.BlockSpec((bm, bn), lambda i, j: (i, j))
    vspec = pl.BlockSpec((1, bn), lambda i, j: (0, j))
    return pl.pallas_call(
        _rec_in_body,
        grid=(R // bm, nb),
        in_specs=[pl.BlockSpec((bm, K), lambda i, j: (i, 0)),
                  wspec(0), wspec(1), wspec(2), wspec(3), vspec, vspec],
        out_specs=[ospec, ospec, ospec, ospec],
        out_shape=[jax.ShapeDtypeStruct((R, D), BF16),
                   jax.ShapeDtypeStruct((R, D), F32),
                   jax.ShapeDtypeStruct((R, D), BF16),
                   jax.ShapeDtypeStruct((R, D), BF16)],
        compiler_params=_params(("parallel", "arbitrary")),
        name="rec_in_proj",
    )(h, w_in, w_in, w_in, w_in, log_lb.reshape(1, D), log_1m_lb.reshape(1, D))


def _segment_tail(cum, w):
    C, W = cum.shape
    if 2 * w >= F32_ROWS * 2:
        pieces = [jnp.broadcast_to(cum[r:r + 1, :], (2 * w, W))
                  for r in range(w - 1, C, 2 * w)]
        return pieces[0] if len(pieces) == 1 else jnp.concatenate(pieces, axis=0)
    tiles = cum.reshape(C // F32_ROWS, F32_ROWS, W)
    sub = lax.broadcasted_iota(jnp.int32, (1, F32_ROWS, 1), 1)
    out = None
    for r in reversed(range(w - 1, F32_ROWS, 2 * w)):
        piece = jnp.broadcast_to(tiles[:, r:r + 1, :], tiles.shape)
        out = piece if out is None else jnp.where(sub < r + w + 1, piece, out)
    return out.reshape(C, W)


def _rec_body(q_ref, lf_ref, v_ref, gs_ref, gain_ref, o_ref, state_ref):
    c = pl.program_id(1)
    C = q_ref.shape[0]
    D = REC_DIM
    contract_last = (((1,), (1,)), ((), ()))
    contract_first = (((0,), (0,)), ((), ()))

    @pl.when(c == 0)
    def _():
        state_ref[...] = jnp.zeros_like(state_ref)

    row = lax.broadcasted_iota(jnp.int32, (C, 1), 0)
    g = jnp.where(c * C + row >= FRAME_PAD, lf_ref[...], 0.0)
    kk = 1.0 - jnp.exp(g)
    q16 = q_ref[...]
    q = q16.astype(F32)
    k16 = kk.astype(BF16)

    ti = lax.broadcasted_iota(jnp.int32, (C, C), 0)
    si = lax.broadcasted_iota(jnp.int32, (C, C), 1)
    bit_diff = jnp.where(ti > si, ti ^ si, 0)
    cum = g
    levels = []
    w = 1
    while w < C:
        upper = (row & w) != 0
        tail = _segment_tail(cum, w)
        f = jnp.exp(jnp.where(upper, cum, tail - cum))
        pair_mask = (bit_diff >= w) & (bit_diff < 2 * w)
        levels.append(((q * f).astype(BF16), (kk * f).astype(BF16), pair_mask))
        cum = cum + jnp.where(upper, tail, 0.0)
        w *= 2
    b = cum
    same_row = ti == si

    b_last = b[C - 1:C, :]
    q_inter = (q * jnp.exp(b)).astype(BF16)
    k_state = (kk * jnp.exp(b_last - b)).astype(BF16)
    state_decay = jnp.exp(b_last)

    v = v_ref[...]
    gain = gain_ref[...]
    for h in range(REC_HEADS):
        sl = slice(h * D, (h + 1) * D)
        st = state_ref[h]
        att = jnp.where(same_row,
                        lax.dot_general(q16[:, sl], k16[:, sl], contract_last,
                                        preferred_element_type=F32), 0.0)
        for q_lvl, k_lvl, pair_mask in levels:
            att = jnp.where(pair_mask,
                            lax.dot_general(q_lvl[:, sl], k_lvl[:, sl], contract_last,
                                            preferred_element_type=F32), att)
        o = (lax.dot_general(q_inter[:, sl], st.astype(BF16), contract_last,
                             preferred_element_type=F32)
             + jnp.dot(att.astype(BF16), v[:, sl], preferred_element_type=F32))
        upd = lax.dot_general(v[:, sl], k_state[:, sl], contract_first,
                              preferred_element_type=F32)
        state_ref[h] = state_decay[:, sl] * st + upd
        ms = jnp.mean(o * o, axis=-1, keepdims=True)
        y = o * lax.rsqrt(ms + EPS) * gain * gs_ref[:, sl].astype(F32)
        o_ref[:, sl] = y.astype(o_ref.dtype)


def _rec_mixer(qs, log_f, v, gs, out_gain, batch, T):
    R, W = qs.shape
    C = REC_CHUNK
    n_chunks = T // C
    spec = pl.BlockSpec((C, W), lambda b, c: (b * n_chunks + c, 0))
    return pl.pallas_call(
        _rec_body,
        grid=(batch, n_chunks),
        in_specs=[spec, spec, spec, spec, pl.BlockSpec((1, REC_DIM), lambda b, c: (0, 0))],
        out_specs=spec,
        out_shape=jax.ShapeDtypeStruct((R, W), BF16),
        scratch_shapes=[pltpu.VMEM((REC_HEADS, REC_DIM, REC_DIM), F32)],
        compiler_params=_params(("parallel", "arbitrary")),
        name="rec_mixer",
    )(qs, log_f, v, gs, out_gain.reshape(1, REC_DIM))


def _att_in_weight(w_in):
    d = w_in.shape[0]
    half = MLA_ROPE_DIM // 2
    w_kr = w_in[:, COL_KR:COL_KR + MLA_ROPE_DIM]
    rot = jnp.concatenate([-w_kr[:, half:], w_kr[:, :half]], axis=1)
    left = jnp.zeros((d, MLA_NOPE_DIM), w_in.dtype)
    right = jnp.zeros((d, LANES - MLA_NOPE_DIM - MLA_ROPE_DIM), w_in.dtype)
    return jnp.concatenate([w_in[:, :COL_KR], left, w_kr, right, left, rot, right], axis=1)


def _mla_q_weights(w_uq):
    r = w_uq.shape[0]
    half = MLA_ROPE_DIM // 2
    w = w_uq.reshape(r, MLA_HEADS, MLA_NOPE_DIM + MLA_ROPE_DIM)
    nope, rope = w[..., :MLA_NOPE_DIM], w[..., MLA_NOPE_DIM:]
    rot = jnp.concatenate([-rope[..., half:], rope[..., :half]], axis=-1)
    fill = jnp.zeros((r, MLA_HEADS, LANES - MLA_NOPE_DIM - MLA_ROPE_DIM), w.dtype)
    plain = jnp.concatenate([nope, rope, fill], axis=-1).reshape(r, MLA_HEADS * LANES)
    rotated = jnp.concatenate([jnp.zeros_like(nope), rot, fill], axis=-1).reshape(r, MLA_HEADS * LANES)
    return plain, rotated


def _mla_kv_weights(w_ukv):
    r = w_ukv.shape[0]
    w = w_ukv.reshape(r, MLA_HEADS, MLA_NOPE_DIM + MLA_V_DIM)
    k_nope, v = w[..., :MLA_NOPE_DIM], w[..., MLA_NOPE_DIM:]
    wk = jnp.concatenate([k_nope, jnp.zeros((r, MLA_HEADS, LANES - MLA_NOPE_DIM), w.dtype)], axis=-1)
    return wk.reshape(r, MLA_HEADS * LANES), v.reshape(r, MLA_HEADS * MLA_V_DIM)


def _rope_tables(T):
    half = MLA_ROPE_DIM // 2
    inv_freq = ROPE_THETA ** (-2.0 * jnp.arange(half, dtype=F32) / MLA_ROPE_DIM)
    pos = jnp.maximum(jnp.arange(T) - FRAME_PAD, 0).astype(F32)
    ang = pos[:, None] * inv_freq[None, :]
    cos, sin = jnp.cos(ang), jnp.sin(ang)
    left = jnp.ones((T, MLA_NOPE_DIM), F32)
    right = jnp.zeros((T, LANES - MLA_NOPE_DIM - MLA_ROPE_DIM), F32)
    cos_t = jnp.concatenate([left, cos, cos, right], axis=1)
    sin_t = jnp.concatenate([0.0 * left, sin, sin, right], axis=1)
    spare = (jnp.arange(LANES) == MLA_NOPE_DIM + MLA_ROPE_DIM).astype(F32)[None, :]
    is_pad = (jnp.arange(T) < FRAME_PAD).astype(F32)[:, None]
    q_bias = jnp.broadcast_to(spare, (T, LANES))
    k_bias = NEG * is_pad * spare
    return cos_t, sin_t, q_bias, k_bias


def kernel(x, meta_tokens, mix_norm, ffn_norm, final_norm, att_w_in, att_sinks, mla_q_norm,
           mla_w_uq, mla_kv_norm, mla_w_ukv, att_w_out, rec_w_in, rec_lower_bounds, rec_out_norm,
           rec_w_out, ffn_w_up, ffn_w_gate, ffn_conv_w, ffn_conv_b, ffn_w_down):
    B, S, D = x.shape
    depth = mix_norm.shape[0]
    T = FRAME_PAD + N_META + S
    assert T % ATT_BLOCK == 0 and T % REC_CHUNK == 0
    R = B * T

    meta = jnp.broadcast_to(meta_tokens[None].astype(x.dtype), (B, N_META, D))
    hs = jnp.concatenate([jnp.zeros((B, FRAME_PAD, D), x.dtype), meta, x], axis=1).reshape(R, D)

    cos_t, sin_t, q_bias, k_bias = _rope_tables(T)
    q_scale = (MLA_NOPE_DIM + MLA_ROPE_DIM) ** -0.5 * LOG2_E
    sm = jax.nn.softmax(rec_lower_bounds.astype(F32), axis=0)
    lower = jnp.cumsum(sm.at[0].set(0.0), axis=0)
    log_lb, log_1m_lb = jnp.log(lower), jnp.log1p(-lower)

    bq = _pick(T, (384, 128))
    t_blocks = T // bq

    h = _rmsnorm(hs, mix_norm[0], BF16)
    for layer in range(depth):
        idx = layer // 2
        if layer % 2 == 0:
            z = _matmul(h, _att_in_weight(att_w_in[idx]).astype(BF16), BF16)
            o_a = _swa_attention(z, att_sinks[idx].astype(F32), B, T // ATT_BLOCK)
            wq, wq_rot = _mla_q_weights(mla_w_uq[idx])
            wk, wv = _mla_kv_weights(mla_w_ukv[idx])
            q = _mla_q_proj(z, mla_q_norm[idx], wq.astype(BF16), wq_rot.astype(BF16),
                            cos_t * q_scale, sin_t * q_scale, q_bias, t_blocks, bq)
            k, v = _mla_kv_proj(z, mla_kv_norm[idx], wk.astype(BF16), wv.astype(BF16),
                                cos_t, sin_t, k_bias, t_blocks, bq)
            o_b = _mla_attention(q, k, v, B, T, _pick(T, (1408, 384, 128)))
            mixed = jnp.concatenate([o_a, o_b], axis=1)
            w_out = att_w_out[idx]
        else:
            qs, log_f, iv, gs = _rec_in_proj(h, rec_w_in[idx].astype(BF16), log_lb[idx], log_1m_lb[idx])
            mixed = _rec_mixer(qs, log_f, iv, gs, rec_out_norm[idx], B, T)
            w_out = rec_w_out[idx]
        hs, h = _proj_residual_norm(mixed, w_out.astype(BF16), hs, ffn_norm[layer], BF16)
        g = _ffn_up(h, ffn_w_up[layer].astype(BF16), ffn_w_gate[layer].astype(BF16),
                    ffn_conv_w[layer], ffn_conv_b[layer])
        w_down = ffn_w_down[layer].astype(BF16)
        if layer + 1 < depth:
            hs, h = _proj_residual_norm(g, w_down, hs, mix_norm[layer + 1], BF16)
        else:
            (out,) = _proj_residual_norm(g, w_down, hs, final_norm, x.dtype, emit_residual=False)
    return out.reshape(B, T, D)[:, FRAME_PAD + N_META:]
```

```python
import functools

import jax
import jax.numpy as jnp
from jax import lax
from jax.experimental import pallas as pl
from jax.experimental.pallas import tpu as pltpu

F32 = jnp.float32
BF16 = jnp.bfloat16

N_META = 16
EPS = 1e-6
NEG = -1e30
LOG2_E = 1.4426950408889634
ATT_BLOCK = 128
SWA_HEADS = 16
SWA_KV_HEADS = 4
SWA_HEAD_DIM = 64
MLA_HEADS = 16
MLA_Q_RANK = 512
MLA_KV_RANK = 256
MLA_NOPE_DIM = 64
MLA_ROPE_DIM = 32
MLA_V_DIM = 64
ROPE_THETA = 10000.0
REC_HEADS = 16
REC_DIM = 128
CONV_WIDTH = 3

LANES = 128
MXU_COLS = 256
F32_ROWS = 8
BF16_ROWS = 16
FRAME_PAD = (-N_META) % ATT_BLOCK
REC_CHUNK = 128
VMEM_LIMIT = 56 * 1024 * 1024

SWA_Q = SWA_HEADS * SWA_HEAD_DIM
SWA_KV = SWA_KV_HEADS * SWA_HEAD_DIM
COL_QA = 0
COL_KA = SWA_Q
COL_VA = SWA_Q + SWA_KV
COL_CQ = SWA_Q + 2 * SWA_KV
COL_CKV = COL_CQ + MLA_Q_RANK
COL_KR = COL_CKV + MLA_KV_RANK
COL_KRROT = COL_KR + LANES
ATT_IN_PAD = COL_KRROT + LANES


def _pick(n, candidates):
    for c in candidates:
        if n % c == 0:
            return c
    raise ValueError(f"no block size in {candidates} divides {n}")


def _params(sem):
    return pltpu.CompilerParams(dimension_semantics=sem, vmem_limit_bytes=VMEM_LIMIT)


def _silu(x):
    return x * (1.0 / (1.0 + jnp.exp(-x)))


def _rmsnorm_body(x_ref, g_ref, o_ref):
    x = x_ref[...]
    ms = jnp.mean(x * x, axis=-1, keepdims=True)
    o_ref[...] = (x * lax.rsqrt(ms + EPS) * g_ref[...]).astype(o_ref.dtype)


def _rmsnorm(x, gain, out_dtype):
    R, D = x.shape
    bm = _pick(R, (768, 512, 384, 256, 128))
    return pl.pallas_call(
        _rmsnorm_body,
        grid=(R // bm,),
        in_specs=[pl.BlockSpec((bm, D), lambda i: (i, 0)),
                  pl.BlockSpec((1, D), lambda i: (0, 0))],
        out_specs=pl.BlockSpec((bm, D), lambda i: (i, 0)),
        out_shape=jax.ShapeDtypeStruct((R, D), out_dtype),
        compiler_params=_params(("parallel",)),
        name="rmsnorm",
    )(x, gain.reshape(1, D))


def _matmul_body(x_ref, w_ref, o_ref):
    o_ref[...] = jnp.dot(x_ref[...], w_ref[...], preferred_element_type=F32).astype(o_ref.dtype)


def _matmul(x, w, out_dtype):
    R, K = x.shape
    N = w.shape[1]
    bm = _pick(R, (1408, 768, 512, 384, 256, 128))
    bn = _pick(N, (1280, 1024, 512, 256, 128))
    return pl.pallas_call(
        _matmul_body,
        grid=(R // bm, N // bn),
        in_specs=[pl.BlockSpec((bm, K), lambda i, j: (i, 0)),
                  pl.BlockSpec((K, bn), lambda i, j: (0, j))],
        out_specs=pl.BlockSpec((bm, bn), lambda i, j: (i, j)),
        out_shape=jax.ShapeDtypeStruct((R, N), out_dtype),
        compiler_params=_params(("parallel", "arbitrary")),
        name="att_in_proj",
    )(x, w)


def _proj_res_body(x_ref, w_ref, res_ref, g_ref, *out_refs):
    *hs_refs, h_ref = out_refs
    y = res_ref[...] + jnp.dot(x_ref[...], w_ref[...], preferred_element_type=F32)
    for hs_ref in hs_refs:
        hs_ref[...] = y
    ms = jnp.mean(y * y, axis=-1, keepdims=True)
    h_ref[...] = (y * lax.rsqrt(ms + EPS) * g_ref[...]).astype(h_ref.dtype)


def _proj_residual_norm(x, w_stack, layer, res, gain, h_dtype, emit_residual=True):
    R, K = x.shape
    N = w_stack.shape[2]
    bm = _pick(R, (384, 256, 128))
    row_spec = pl.BlockSpec((bm, N), lambda i: (i, 0))
    out_specs = [row_spec]
    out_shape = [jax.ShapeDtypeStruct((R, N), h_dtype)]
    if emit_residual:
        out_specs = [row_spec] + out_specs
        out_shape = [jax.ShapeDtypeStruct((R, N), F32)] + out_shape
    return pl.pallas_call(
        _proj_res_body,
        grid=(R // bm,),
        in_specs=[pl.BlockSpec((bm, K), lambda i: (i, 0)),
                  pl.BlockSpec((None, K, N), lambda i: (layer, 0, 0),
                               pipeline_mode=pl.Buffered(1)),
                  row_spec,
                  pl.BlockSpec((1, N), lambda i: (0, 0))],
        out_specs=out_specs,
        out_shape=out_shape,
        compiler_params=_params(("parallel",)),
        name="proj_residual_norm",
    )(x, w_stack, res, gain.reshape(1, N))


def _ffn_up_body(x_ref, halo_ref, wu_ref, wg_ref, cw_ref, cb_ref, o_ref, lhs_ref, a_ref):
    bm = x_ref.shape[0]
    halo = halo_ref.shape[0]

    @pl.when(pl.program_id(1) == 0)
    def _():
        lhs_ref[0:halo, :] = halo_ref[...]
        lhs_ref[halo:, :] = x_ref[...]

    a_ref[...] = jnp.dot(lhs_ref[...], wg_ref[...].astype(BF16), preferred_element_type=F32)
    u = jnp.dot(x_ref[...], wu_ref[...].astype(BF16), preferred_element_type=F32)
    cw = cw_ref[...]
    a = cb_ref[...] + cw[CONV_WIDTH - 1:CONV_WIDTH, :] * a_ref[halo:, :]
    for j in range(CONV_WIDTH - 1):
        back = CONV_WIDTH - 1 - j
        a = a + cw[j:j + 1, :] * a_ref[pl.ds(halo - back, bm), :]
    o_ref[...] = (_silu(a) * u).astype(o_ref.dtype)


def _ffn_up(h, w_up, w_gate, conv_w, conv_b, layer):
    R, K = h.shape
    N = w_up.shape[2]
    bm = _pick(R, (1408, 768, 512, 384, 256, 128))
    bn = _pick(N, (512, 256, 128))
    halo = BF16_ROWS
    hb = bm // halo
    return pl.pallas_call(
        _ffn_up_body,
        grid=(R // bm, N // bn),
        in_specs=[pl.BlockSpec((bm, K), lambda i, j: (i, 0)),
                  pl.BlockSpec((halo, K), lambda i, j: (jnp.maximum(i * hb - 1, 0), 0)),
                  pl.BlockSpec((None, K, bn), lambda i, j: (layer, 0, j)),
                  pl.BlockSpec((None, K, bn), lambda i, j: (layer, 0, j)),
                  pl.BlockSpec((None, CONV_WIDTH, bn), lambda i, j: (layer, 0, j)),
                  pl.BlockSpec((None, 1, bn), lambda i, j: (layer, 0, j))],
        out_specs=pl.BlockSpec((bm, bn), lambda i, j: (i, j)),
        out_shape=jax.ShapeDtypeStruct((R, N), BF16),
        scratch_shapes=[pltpu.VMEM((bm + halo, K), BF16),
                        pltpu.VMEM((bm + halo, bn), F32)],
        compiler_params=_params(("parallel", "arbitrary")),
        name="ffn_up",
    )(h, h, w_up, w_gate, conv_w, conv_b.reshape(conv_b.shape[0], 1, N))


def _swa_body(sink_ref, q_ref, kp_ref, kc_ref, vp_ref, vc_ref, o_ref):
    n = pl.program_id(1)
    blk = q_ref.shape[0]
    dh = SWA_HEAD_DIM
    group = SWA_HEADS // SWA_KV_HEADS
    qi = lax.broadcasted_iota(jnp.int32, (blk, 2 * blk), 0)
    kj = lax.broadcasted_iota(jnp.int32, (blk, 2 * blk), 1)
    rel = qi + blk - kj
    key_pos = n * blk + kj - blk
    mask = (rel >= 0) & (rel < blk) & (key_pos >= FRAME_PAD)
    for g in range(SWA_KV_HEADS):
        ks = slice(g * dh, (g + 1) * dh)
        k = jnp.concatenate([kp_ref[:, ks], kc_ref[:, ks]], axis=0)
        v = jnp.concatenate([vp_ref[:, ks], vc_ref[:, ks]], axis=0)
        for hh in range(group):
            h = g * group + hh
            hs = slice(h * dh, (h + 1) * dh)
            q = q_ref[:, hs] * (dh ** -0.5)
            s = lax.dot_general(q, k, (((1,), (1,)), ((), ())), preferred_element_type=F32)
            s = jnp.where(mask, s, NEG)
            sink = sink_ref[h]
            m = jnp.maximum(jnp.max(s, axis=-1, keepdims=True), sink)
            p = jnp.exp(s - m)
            den = jnp.sum(p, axis=-1, keepdims=True) + jnp.exp(sink - m)
            o = jnp.dot(p.astype(BF16), v, preferred_element_type=F32) / den
            o_ref[:, hs] = o.astype(o_ref.dtype)


def _swa_attention(z, sinks, batch, n_blocks):
    R = z.shape[0]
    blk = ATT_BLOCK
    kcol = COL_KA // SWA_KV
    vcol = COL_VA // SWA_KV

    def cur(b, n):
        return b * n_blocks + n

    def prev(b, n):
        return b * n_blocks + jnp.maximum(n - 1, 0)

    return pl.pallas_call(
        _swa_body,
        grid=(batch, n_blocks),
        in_specs=[pl.BlockSpec(memory_space=pltpu.SMEM),
                  pl.BlockSpec((blk, SWA_Q), lambda b, n: (cur(b, n), 0)),
                  pl.BlockSpec((blk, SWA_KV), lambda b, n: (prev(b, n), kcol)),
                  pl.BlockSpec((blk, SWA_KV), lambda b, n: (cur(b, n), kcol)),
                  pl.BlockSpec((blk, SWA_KV), lambda b, n: (prev(b, n), vcol)),
                  pl.BlockSpec((blk, SWA_KV), lambda b, n: (cur(b, n), vcol))],
        out_specs=pl.BlockSpec((blk, SWA_Q), lambda b, n: (cur(b, n), 0)),
        out_shape=jax.ShapeDtypeStruct((R, SWA_Q), BF16),
        compiler_params=_params(("parallel", "arbitrary")),
        name="swa_attention",
    )(sinks, z, z, z, z, z)


def _mla_q_body(cq_ref, g_ref, w_ref, wrot_ref, cos_ref, sin_ref, bias_ref, o_ref):
    x = cq_ref[...].astype(F32)
    ms = jnp.mean(x * x, axis=-1, keepdims=True)
    h = (x * lax.rsqrt(ms + EPS) * g_ref[...]).astype(BF16)
    a = jnp.dot(h, w_ref[...], preferred_element_type=F32)
    ar = jnp.dot(h, wrot_ref[...], preferred_element_type=F32)
    c = cos_ref[...]
    s = sin_ref[...]
    bias = bias_ref[...]
    for hh in range(MLA_HEADS):
        sl = slice(hh * LANES, (hh + 1) * LANES)
        o_ref[:, sl] = (a[:, sl] * c + ar[:, sl] * s + bias).astype(o_ref.dtype)


def _mla_q_proj(z, gain, w, wrot, cos_t, sin_t, bias_t, t_blocks, bm):
    R = z.shape[0]
    N = w.shape[1]
    table = pl.BlockSpec((bm, LANES), lambda i: (i % t_blocks, 0))
    return pl.pallas_call(
        _mla_q_body,
        grid=(R // bm,),
        in_specs=[pl.BlockSpec((bm, MLA_Q_RANK), lambda i: (i, COL_CQ // MLA_Q_RANK)),
                  pl.BlockSpec((1, MLA_Q_RANK), lambda i: (0, 0)),
                  pl.BlockSpec((MLA_Q_RANK, N), lambda i: (0, 0)),
                  pl.BlockSpec((MLA_Q_RANK, N), lambda i: (0, 0)),
                  table, table, table],
        out_specs=pl.BlockSpec((bm, N), lambda i: (i, 0)),
        out_shape=jax.ShapeDtypeStruct((R, N), BF16),
        compiler_params=_params(("parallel",)),
        name="mla_q_proj",
    )(z, gain.reshape(1, -1), w, wrot, cos_t, sin_t, bias_t)


def _mla_kv_body(ckv_ref, kr_ref, krrot_ref, g_ref, wk_ref, wv_ref, cos_ref, sin_ref, bias_ref,
                 k_ref, v_ref):
    x = ckv_ref[...].astype(F32)
    ms = jnp.mean(x * x, axis=-1, keepdims=True)
    h = (x * lax.rsqrt(ms + EPS) * g_ref[...]).astype(BF16)
    ak = jnp.dot(h, wk_ref[...], preferred_element_type=F32)
    v_ref[...] = jnp.dot(h, wv_ref[...], preferred_element_type=F32).astype(v_ref.dtype)
    rope = (kr_ref[...].astype(F32) * cos_ref[...] + krrot_ref[...].astype(F32) * sin_ref[...]
            + bias_ref[...])
    for hh in range(MLA_HEADS):
        sl = slice(hh * LANES, (hh + 1) * LANES)
        k_ref[:, sl] = (ak[:, sl] + rope).astype(k_ref.dtype)


def _mla_kv_proj(z, gain, wk, wv, cos_t, sin_t, bias_t, t_blocks, bm):
    R = z.shape[0]
    table = pl.BlockSpec((bm, LANES), lambda i: (i % t_blocks, 0))
    return pl.pallas_call(
        _mla_kv_body,
        grid=(R // bm,),
        in_specs=[pl.BlockSpec((bm, MLA_KV_RANK), lambda i: (i, COL_CKV // MLA_KV_RANK)),
                  pl.BlockSpec((bm, LANES), lambda i: (i, COL_KR // LANES)),
                  pl.BlockSpec((bm, LANES), lambda i: (i, COL_KRROT // LANES)),
                  pl.BlockSpec((1, MLA_KV_RANK), lambda i: (0, 0)),
                  pl.BlockSpec(wk.shape, lambda i: (0, 0)),
                  pl.BlockSpec(wv.shape, lambda i: (0, 0)),
                  table, table, table],
        out_specs=[pl.BlockSpec((bm, wk.shape[1]), lambda i: (i, 0)),
                   pl.BlockSpec((bm, wv.shape[1]), lambda i: (i, 0))],
        out_shape=[jax.ShapeDtypeStruct((R, wk.shape[1]), BF16),
                   jax.ShapeDtypeStruct((R, wv.shape[1]), BF16)],
        compiler_params=_params(("parallel",)),
        name="mla_kv_proj",
    )(z, z, z, gain.reshape(1, -1), wk, wv, cos_t, sin_t, bias_t)


def _lane_blocks(s):
    return [s[:, c * LANES:(c + 1) * LANES] for c in range(s.shape[1] // LANES)]


def _mla_attn_body(q_ref, k_ref, v_ref, o_ref, vext_ref, m_ref, acc_ref, *, bq, heads):
    i = pl.program_id(2)
    dv = MLA_V_DIM
    kb = MXU_COLS
    T = k_ref.shape[0]
    lane = lax.broadcasted_iota(jnp.int32, (1, LANES), 1)
    own = [(lane >= hh * dv) & (lane < (hh + 1) * dv) for hh in range(heads)]

    @pl.when(i == 0)
    def _():
        for c in range(T // bq):
            rows = slice(c * bq, (c + 1) * bq)
            vv = v_ref[rows, :]
            for hh in range(heads):
                vext_ref[hh, rows, :] = jnp.where(own[hh], vv, jnp.ones_like(vv))

    def scores(hh, start, size, row0=0):
        sl = slice(hh * LANES, (hh + 1) * LANES)
        return lax.dot_general(q_ref[row0:, sl], k_ref[pl.ds(start, size), sl],
                               (((1,), (1,)), ((), ())), preferred_element_type=F32)

    def online(hh, start, size, row0=0, mask=None):
        s = scores(hh, start, size, row0)
        if mask is not None:
            s = mask(s)
        blocks = _lane_blocks(s)
        top = blocks[0]
        for blk in blocks[1:]:
            top = jnp.maximum(top, blk)
        m = m_ref[hh, row0:, :]
        m_new = jnp.maximum(m, jnp.max(top, axis=-1, keepdims=True))
        p = jnp.concatenate([jnp.exp2(blk - m_new) for blk in blocks], axis=1).astype(BF16)
        m_ref[hh, row0:, :] = m_new
        acc_ref[hh, row0:, :] = (jnp.exp2(m - m_new) * acc_ref[hh, row0:, :]
                                 + jnp.dot(p, vext_ref[hh, pl.ds(start, size), :],
                                           preferred_element_type=F32))

    m_ref[...] = jnp.full(m_ref.shape, NEG, F32)
    acc_ref[...] = jnp.zeros(acc_ref.shape, F32)

    visible = i * bq
    big = 2 * kb
    n_big = visible // big
    left = visible - n_big * big

    def big_tile(j, carry):
        for hh in range(heads):
            online(hh, pl.multiple_of(j * big, big), big)
        return carry

    lax.fori_loop(0, n_big, big_tile, 0)

    @pl.when(left >= kb)
    def _():
        for hh in range(heads):
            online(hh, pl.multiple_of(n_big * big, kb), kb)

    @pl.when(left % kb > 0)
    def _():
        for hh in range(heads):
            online(hh, pl.multiple_of(visible - LANES, LANES), LANES)

    for k0 in range(0, bq, big):
        size = min(big, bq - k0)
        lower = (lax.broadcasted_iota(jnp.int32, (size, 1), 0)
                 >= lax.broadcasted_iota(jnp.int32, (1, size), 1))

        def causal(s, size=size, lower=lower):
            head = jnp.where(lower, s[:size], NEG)
            return head if s.shape[0] == size else jnp.concatenate([head, s[size:]], axis=0)

        for hh in range(heads):
            online(hh, pl.multiple_of(visible + k0, LANES), size, row0=k0, mask=causal)

    out = jnp.zeros((bq, LANES), F32)
    for hh in range(heads):
        acc = acc_ref[hh]
        denom = pltpu.roll(acc, dv, axis=1)
        out = jnp.where(own[hh], acc / denom, out)
    q_pos = i * bq + lax.broadcasted_iota(jnp.int32, (bq, 1), 0)
    o_ref[...] = jnp.where(q_pos >= FRAME_PAD, out, 0.0).astype(o_ref.dtype)


def _mla_attention(q, k, v, batch, T, bq):
    R = q.shape[0]
    heads = LANES // MLA_V_DIM
    groups = MLA_HEADS // heads
    t_blocks = T // bq
    wq = heads * LANES
    return pl.pallas_call(
        functools.partial(_mla_attn_body, bq=bq, heads=heads),
        grid=(batch, groups, t_blocks),
        in_specs=[pl.BlockSpec((bq, wq), lambda b, g, i: (b * t_blocks + i, g)),
                  pl.BlockSpec((T, wq), lambda b, g, i: (b, g)),
                  pl.BlockSpec((T, LANES), lambda b, g, i: (b, g))],
        out_specs=pl.BlockSpec((bq, LANES), lambda b, g, i: (b * t_blocks + i, g)),
        out_shape=jax.ShapeDtypeStruct((R, MLA_HEADS * MLA_V_DIM), BF16),
        scratch_shapes=[pltpu.VMEM((heads, T, LANES), BF16),
                        pltpu.VMEM((heads, bq, LANES), F32),
                        pltpu.VMEM((heads, bq, LANES), F32)],
        compiler_params=_params(("parallel", "parallel", "arbitrary")),
        name="mla_attention",
    )(q, k, v)


def _rec_in_body(x_ref, wq_ref, wf_ref, wi_ref, wg_ref, la_ref, lc_ref,
                 q_ref, lf_ref, i_ref, g_ref):
    x = x_ref[...]
    q = jnp.dot(x, wq_ref[...], preferred_element_type=F32)
    q_ref[...] = _silu(q).astype(q_ref.dtype)
    f = jnp.dot(x, wf_ref[...], preferred_element_type=F32)
    log_sig = jnp.minimum(f, 0.0) - jnp.log1p(jnp.exp(-jnp.abs(f)))
    y = lc_ref[...] + log_sig
    la = la_ref[...]
    m = jnp.maximum(la, y)
    lf_ref[...] = m + jnp.log(jnp.exp(la - m) + jnp.exp(y - m))
    i_ref[...] = jnp.dot(x, wi_ref[...], preferred_element_type=F32).astype(i_ref.dtype)
    g = jnp.dot(x, wg_ref[...], preferred_element_type=F32)
    g_ref[...] = _silu(g).astype(g_ref.dtype)


def _rec_in_proj(h, w_in, layer, log_lb, log_1m_lb):
    R, K = h.shape
    D = w_in.shape[2] // 4
    bm = _pick(R, (768, 512, 384, 256, 128))
    bn = _pick(D, (512, 256, 128))
    nb = D // bn

    def wspec(sec):
        return pl.BlockSpec((None, K, bn), lambda i, j: (layer, 0, sec * nb + j))

    ospec = pl.BlockSpec((bm, bn), lambda i, j: (i, j))
    vspec = pl.BlockSpec((1, bn), lambda i, j: (0, j))
    return pl.pallas_call(
        _rec_in_body,
        grid=(R // bm, nb),
        in_specs=[pl.BlockSpec((bm, K), lambda i, j: (i, 0)),
                  wspec(0), wspec(1), wspec(2), wspec(3), vspec, vspec],
        out_specs=[ospec, ospec, ospec, ospec],
        out_shape=[jax.ShapeDtypeStruct((R, D), BF16),
                   jax.ShapeDtypeStruct((R, D), F32),
                   jax.ShapeDtypeStruct((R, D), BF16),
                   jax.ShapeDtypeStruct((R, D), BF16)],
        compiler_params=_params(("parallel", "arbitrary")),
        name="rec_in_proj",
    )(h, w_in, w_in, w_in, w_in, log_lb.reshape(1, D), log_1m_lb.reshape(1, D))


def _segment_tail(cum, w):
    C, W = cum.shape
    if 2 * w >= F32_ROWS * 2:
        pieces = [jnp.broadcast_to(cum[r:r + 1, :], (2 * w, W))
                  for r in range(w - 1, C, 2 * w)]
        return pieces[0] if len(pieces) == 1 else jnp.concatenate(pieces, axis=0)
    tiles = cum.reshape(C // F32_ROWS, F32_ROWS, W)
    sub = lax.broadcasted_iota(jnp.int32, (1, F32_ROWS, 1), 1)
    out = None
    for r in reversed(range(w - 1, F32_ROWS, 2 * w)):
        piece = jnp.broadcast_to(tiles[:, r:r + 1, :], tiles.shape)
        out = piece if out is None else jnp.where(sub < r + w + 1, piece, out)
    return out.reshape(C, W)


def _rec_body(q_ref, lf_ref, v_ref, gs_ref, gain_ref, o_ref, state_ref):
    c = pl.program_id(1)
    C = q_ref.shape[0]
    D = REC_DIM
    contract_last = (((1,), (1,)), ((), ()))
    contract_first = (((0,), (0,)), ((), ()))

    @pl.when(c == 0)
    def _():
        state_ref[...] = jnp.zeros_like(state_ref)

    row = lax.broadcasted_iota(jnp.int32, (C, 1), 0)
    g = jnp.where(c * C + row >= FRAME_PAD, lf_ref[...], 0.0)
    kk = 1.0 - jnp.exp(g)
    q16 = q_ref[...]
    q = q16.astype(F32)
    k16 = kk.astype(BF16)

    ti = lax.broadcasted_iota(jnp.int32, (C, C), 0)
    si = lax.broadcasted_iota(jnp.int32, (C, C), 1)
    bit_diff = jnp.where(ti > si, ti ^ si, 0)
    cum = g
    levels = []
    w = 1
    while w < C:
        upper = (row & w) != 0
        tail = _segment_tail(cum, w)
        f = jnp.exp(jnp.where(upper, cum, tail - cum))
        pair_mask = (bit_diff >= w) & (bit_diff < 2 * w)
        levels.append(((q * f).astype(BF16), (kk * f).astype(BF16), pair_mask))
        cum = cum + jnp.where(upper, tail, 0.0)
        w *= 2
    b = cum
    same_row = ti == si

    b_last = b[C - 1:C, :]
    q_inter = (q * jnp.exp(b)).astype(BF16)
    k_state = (kk * jnp.exp(b_last - b)).astype(BF16)
    state_decay = jnp.exp(b_last)

    v = v_ref[...]
    gain = gain_ref[...]
    for h in range(REC_HEADS):
        sl = slice(h * D, (h + 1) * D)
        st = state_ref[h]
        att = jnp.where(same_row,
                        lax.dot_general(q16[:, sl], k16[:, sl], contract_last,
                                        preferred_element_type=F32), 0.0)
        for q_lvl, k_lvl, pair_mask in levels:
            att = jnp.where(pair_mask,
                            lax.dot_general(q_lvl[:, sl], k_lvl[:, sl], contract_last,
                                            preferred_element_type=F32), att)
        o = (lax.dot_general(q_inter[:, sl], st.astype(BF16), contract_last,
                             preferred_element_type=F32)
             + jnp.dot(att.astype(BF16), v[:, sl], preferred_element_type=F32))
        upd = lax.dot_general(v[:, sl], k_state[:, sl], contract_first,
                              preferred_element_type=F32)
        state_ref[h] = state_decay[:, sl] * st + upd
        ms = jnp.mean(o * o, axis=-1, keepdims=True)
        y = o * lax.rsqrt(ms + EPS) * gain * gs_ref[:, sl].astype(F32)
        o_ref[:, sl] = y.astype(o_ref.dtype)


def _rec_mixer(qs, log_f, v, gs, out_gain, batch, T):
    R, W = qs.shape
    C = REC_CHUNK
    n_chunks = T // C
    spec = pl.BlockSpec((C, W), lambda b, c: (b * n_chunks + c, 0))
    return pl.pallas_call(
        _rec_body,
        grid=(batch, n_chunks),
        in_specs=[spec, spec, spec, spec, pl.BlockSpec((1, REC_DIM), lambda b, c: (0, 0))],
        out_specs=spec,
        out_shape=jax.ShapeDtypeStruct((R, W), BF16),
        scratch_shapes=[pltpu.VMEM((REC_HEADS, REC_DIM, REC_DIM), F32)],
        compiler_params=_params(("parallel", "arbitrary")),
        name="rec_mixer",
    )(qs, log_f, v, gs, out_gain.reshape(1, REC_DIM))


def _att_in_weight(w_in):
    d = w_in.shape[0]
    half = MLA_ROPE_DIM // 2
    w_kr = w_in[:, COL_KR:COL_KR + MLA_ROPE_DIM]
    rot = jnp.concatenate([-w_kr[:, half:], w_kr[:, :half]], axis=1)
    left = jnp.zeros((d, MLA_NOPE_DIM), w_in.dtype)
    right = jnp.zeros((d, LANES - MLA_NOPE_DIM - MLA_ROPE_DIM), w_in.dtype)
    return jnp.concatenate([w_in[:, :COL_KR], left, w_kr, right, left, rot, right], axis=1)


def _mla_q_weights(w_uq):
    r = w_uq.shape[0]
    half = MLA_ROPE_DIM // 2
    w = w_uq.reshape(r, MLA_HEADS, MLA_NOPE_DIM + MLA_ROPE_DIM)
    nope, rope = w[..., :MLA_NOPE_DIM], w[..., MLA_NOPE_DIM:]
    rot = jnp.concatenate([-rope[..., half:], rope[..., :half]], axis=-1)
    fill = jnp.zeros((r, MLA_HEADS, LANES - MLA_NOPE_DIM - MLA_ROPE_DIM), w.dtype)
    plain = jnp.concatenate([nope, rope, fill], axis=-1).reshape(r, MLA_HEADS * LANES)
    rotated = jnp.concatenate([jnp.zeros_like(nope), rot, fill], axis=-1).reshape(r, MLA_HEADS * LANES)
    return plain, rotated


def _mla_kv_weights(w_ukv):
    r = w_ukv.shape[0]
    w = w_ukv.reshape(r, MLA_HEADS, MLA_NOPE_DIM + MLA_V_DIM)
    k_nope, v = w[..., :MLA_NOPE_DIM], w[..., MLA_NOPE_DIM:]
    wk = jnp.concatenate([k_nope, jnp.zeros((r, MLA_HEADS, LANES - MLA_NOPE_DIM), w.dtype)], axis=-1)
    return wk.reshape(r, MLA_HEADS * LANES), v.reshape(r, MLA_HEADS * MLA_V_DIM)


def _rope_tables(T):
    half = MLA_ROPE_DIM // 2
    inv_freq = ROPE_THETA ** (-2.0 * jnp.arange(half, dtype=F32) / MLA_ROPE_DIM)
    pos = jnp.maximum(jnp.arange(T) - FRAME_PAD, 0).astype(F32)
    ang = pos[:, None] * inv_freq[None, :]
    cos, sin = jnp.cos(ang), jnp.sin(ang)
    left = jnp.ones((T, MLA_NOPE_DIM), F32)
    right = jnp.zeros((T, LANES - MLA_NOPE_DIM - MLA_ROPE_DIM), F32)
    cos_t = jnp.concatenate([left, cos, cos, right], axis=1)
    sin_t = jnp.concatenate([0.0 * left, sin, sin, right], axis=1)
    spare = (jnp.arange(LANES) == MLA_NOPE_DIM + MLA_ROPE_DIM).astype(F32)[None, :]
    is_pad = (jnp.arange(T) < FRAME_PAD).astype(F32)[:, None]
    q_bias = jnp.broadcast_to(spare, (T, LANES))
    k_bias = NEG * is_pad * spare
    return cos_t, sin_t, q_bias, k_bias


def kernel(x, meta_tokens, mix_norm, ffn_norm, final_norm, att_w_in, att_sinks, mla_q_norm,
           mla_w_uq, mla_kv_norm, mla_w_ukv, att_w_out, rec_w_in, rec_lower_bounds, rec_out_norm,
           rec_w_out, ffn_w_up, ffn_w_gate, ffn_conv_w, ffn_conv_b, ffn_w_down):
    B, S, D = x.shape
    depth = mix_norm.shape[0]
    T = FRAME_PAD + N_META + S
    assert T % ATT_BLOCK == 0 and T % REC_CHUNK == 0
    R = B * T

    meta = jnp.broadcast_to(meta_tokens[None].astype(x.dtype), (B, N_META, D))
    hs = jnp.concatenate([jnp.zeros((B, FRAME_PAD, D), x.dtype), meta, x], axis=1).reshape(R, D)

    cos_t, sin_t, q_bias, k_bias = _rope_tables(T)
    q_scale = (MLA_NOPE_DIM + MLA_ROPE_DIM) ** -0.5 * LOG2_E
    sm = jax.nn.softmax(rec_lower_bounds.astype(F32), axis=0)
    lower = jnp.cumsum(sm.at[0].set(0.0), axis=0)
    log_lb, log_1m_lb = jnp.log(lower), jnp.log1p(-lower)

    bq = _pick(T, (384, 128))
    t_blocks = T // bq

    att_w_out16, rec_w_out16 = att_w_out.astype(BF16), rec_w_out.astype(BF16)
    rec_w_in16, ffn_w_down16 = rec_w_in.astype(BF16), ffn_w_down.astype(BF16)

    h = _rmsnorm(hs, mix_norm[0], BF16)
    for layer in range(depth):
        idx = layer // 2
        if layer % 2 == 0:
            z = _matmul(h, _att_in_weight(att_w_in[idx]).astype(BF16), BF16)
            o_a = _swa_attention(z, att_sinks[idx].astype(F32), B, T // ATT_BLOCK)
            wq, wq_rot = _mla_q_weights(mla_w_uq[idx])
            wk, wv = _mla_kv_weights(mla_w_ukv[idx])
            q = _mla_q_proj(z, mla_q_norm[idx], wq.astype(BF16), wq_rot.astype(BF16),
                            cos_t * q_scale, sin_t * q_scale, q_bias, t_blocks, bq)
            k, v = _mla_kv_proj(z, mla_kv_norm[idx], wk.astype(BF16), wv.astype(BF16),
                                cos_t, sin_t, k_bias, t_blocks, bq)
            o_b = _mla_attention(q, k, v, B, T, _pick(T, (1408, 384, 128)))
            mixed = jnp.concatenate([o_a, o_b], axis=1)
            w_out = att_w_out16
        else:
            qs, log_f, iv, gs = _rec_in_proj(h, rec_w_in16, idx, log_lb[idx], log_1m_lb[idx])
            mixed = _rec_mixer(qs, log_f, iv, gs, rec_out_norm[idx], B, T)
            w_out = rec_w_out16
        hs, h = _proj_residual_norm(mixed, w_out, idx, hs, ffn_norm[layer], BF16)
        g = _ffn_up(h, ffn_w_up, ffn_w_gate, ffn_conv_w, ffn_conv_b, layer)
        if layer + 1 < depth:
            hs, h = _proj_residual_norm(g, ffn_w_down16, layer, hs, mix_norm[layer + 1], BF16)
        else:
            (out,) = _proj_residual_norm(g, ffn_w_down16, layer, hs, final_norm, x.dtype,
                                         emit_residual=False)
    return out.reshape(B, T, D)[:, FRAME_PAD + N_META:]
```

```python
import functools

import jax
import jax.numpy as jnp
from jax import lax
from jax.experimental import pallas as pl
from jax.experimental.pallas import tpu as pltpu

F32 = jnp.float32
BF16 = jnp.bfloat16

N_META = 16
EPS = 1e-6
NEG = -1e30
LOG2_E = 1.4426950408889634
ATT_BLOCK = 128
SWA_HEADS = 16
SWA_KV_HEADS = 4
SWA_HEAD_DIM = 64
MLA_HEADS = 16
MLA_Q_RANK = 512
MLA_KV_RANK = 256
MLA_NOPE_DIM = 64
MLA_ROPE_DIM = 32
MLA_V_DIM = 64
ROPE_THETA = 10000.0
REC_HEADS = 16
REC_DIM = 128
CONV_WIDTH = 3

LANES = 128
MXU_COLS = 256
F32_ROWS = 8
BF16_ROWS = 16
FRAME_PAD = (-N_META) % ATT_BLOCK
REC_CHUNK = 128
VMEM_LIMIT = 56 * 1024 * 1024

SWA_Q = SWA_HEADS * SWA_HEAD_DIM
SWA_KV = SWA_KV_HEADS * SWA_HEAD_DIM
COL_QA = 0
COL_KA = SWA_Q
COL_VA = SWA_Q + SWA_KV
COL_CQ = SWA_Q + 2 * SWA_KV
COL_CKV = COL_CQ + MLA_Q_RANK
COL_KR = COL_CKV + MLA_KV_RANK
COL_KRROT = COL_KR + LANES
ATT_IN_PAD = COL_KRROT + LANES


def _pick(n, candidates):
    for c in candidates:
        if n % c == 0:
            return c
    raise ValueError(f"no block size in {candidates} divides {n}")


def _params(sem):
    return pltpu.CompilerParams(dimension_semantics=sem, vmem_limit_bytes=VMEM_LIMIT)


def _silu(x):
    return x * (1.0 / (1.0 + jnp.exp(-x)))


def _rmsnorm_body(x_ref, g_ref, o_ref):
    x = x_ref[...]
    ms = jnp.mean(x * x, axis=-1, keepdims=True)
    o_ref[...] = (x * lax.rsqrt(ms + EPS) * g_ref[...]).astype(o_ref.dtype)


def _rmsnorm(x, gain, out_dtype):
    R, D = x.shape
    bm = _pick(R, (768, 512, 384, 256, 128))
    return pl.pallas_call(
        _rmsnorm_body,
        grid=(R // bm,),
        in_specs=[pl.BlockSpec((bm, D), lambda i: (i, 0)),
                  pl.BlockSpec((1, D), lambda i: (0, 0))],
        out_specs=pl.BlockSpec((bm, D), lambda i: (i, 0)),
        out_shape=jax.ShapeDtypeStruct((R, D), out_dtype),
        compiler_params=_params(("parallel",)),
        name="rmsnorm",
    )(x, gain.reshape(1, D))


def _matmul_body(x_ref, w_ref, o_ref):
    o_ref[...] = jnp.dot(x_ref[...], w_ref[...], preferred_element_type=F32).astype(o_ref.dtype)


def _matmul(x, w, out_dtype):
    R, K = x.shape
    N = w.shape[1]
    bm = _pick(R, (1408, 768, 512, 384, 256, 128))
    bn = _pick(N, (1280, 1024, 512, 256, 128))
    return pl.pallas_call(
        _matmul_body,
        grid=(R // bm, N // bn),
        in_specs=[pl.BlockSpec((bm, K), lambda i, j: (i, 0)),
                  pl.BlockSpec((K, bn), lambda i, j: (0, j))],
        out_specs=pl.BlockSpec((bm, bn), lambda i, j: (i, j)),
        out_shape=jax.ShapeDtypeStruct((R, N), out_dtype),
        compiler_params=_params(("parallel", "arbitrary")),
        name="att_in_proj",
    )(x, w)


def _proj_res_body(x_ref, w_ref, res_ref, g_ref, *out_refs):
    *hs_refs, h_ref = out_refs
    y = res_ref[...] + jnp.dot(x_ref[...], w_ref[...], preferred_element_type=F32)
    for hs_ref in hs_refs:
        hs_ref[...] = y
    ms = jnp.mean(y * y, axis=-1, keepdims=True)
    h_ref[...] = (y * lax.rsqrt(ms + EPS) * g_ref[...]).astype(h_ref.dtype)


def _proj_residual_norm(x, w_stack, layer, res, gain, h_dtype, emit_residual=True):
    R, K = x.shape
    N = w_stack.shape[2]
    bm = _pick(R, (384, 256, 128))
    row_spec = pl.BlockSpec((bm, N), lambda i: (i, 0))
    out_specs = [row_spec]
    out_shape = [jax.ShapeDtypeStruct((R, N), h_dtype)]
    if emit_residual:
        out_specs = [row_spec] + out_specs
        out_shape = [jax.ShapeDtypeStruct((R, N), F32)] + out_shape
    return pl.pallas_call(
        _proj_res_body,
        grid=(R // bm,),
        in_specs=[pl.BlockSpec((bm, K), lambda i: (i, 0)),
                  pl.BlockSpec((None, K, N), lambda i: (layer, 0, 0),
                               pipeline_mode=pl.Buffered(1)),
                  row_spec,
                  pl.BlockSpec((1, N), lambda i: (0, 0))],
        out_specs=out_specs,
        out_shape=out_shape,
        compiler_params=_params(("parallel",)),
        name="proj_residual_norm",
    )(x, w_stack, res, gain.reshape(1, N))


def _final_proj_norm(x, w_stack, layer, res, gain, batch, T, skip, out_dtype):
    K = x.shape[1]
    N = w_stack.shape[2]
    S = T - skip
    bm = _pick(S, (256, 128))
    nb = S // bm

    def frame_row(b, j):
        return pl.multiple_of(b * T + skip + j * bm, LANES)

    return pl.pallas_call(
        _proj_res_body,
        grid=(batch, nb),
        in_specs=[pl.BlockSpec((pl.Element(bm), pl.Element(K)), lambda b, j: (frame_row(b, j), 0)),
                  pl.BlockSpec((None, K, N), lambda b, j: (layer, 0, 0),
                               pipeline_mode=pl.Buffered(1)),
                  pl.BlockSpec((pl.Element(bm), pl.Element(N)), lambda b, j: (frame_row(b, j), 0)),
                  pl.BlockSpec((1, N), lambda b, j: (0, 0))],
        out_specs=[pl.BlockSpec((bm, N), lambda b, j: (b * nb + j, 0))],
        out_shape=[jax.ShapeDtypeStruct((batch * S, N), out_dtype)],
        compiler_params=_params(("parallel", "parallel")),
        name="final_proj_norm",
    )(x, w_stack, res, gain.reshape(1, N))[0]


def _ffn_up_body(x_ref, halo_ref, wu_ref, wg_ref, cw_ref, cb_ref, o_ref, lhs_ref, a_ref):
    bm = x_ref.shape[0]
    halo = halo_ref.shape[0]

    @pl.when(pl.program_id(1) == 0)
    def _():
        lhs_ref[0:halo, :] = halo_ref[...]
        lhs_ref[halo:, :] = x_ref[...]

    a_ref[...] = jnp.dot(lhs_ref[...], wg_ref[...].astype(BF16), preferred_element_type=F32)
    u = jnp.dot(x_ref[...], wu_ref[...].astype(BF16), preferred_element_type=F32)
    cw = cw_ref[...]
    a = cb_ref[...] + cw[CONV_WIDTH - 1:CONV_WIDTH, :] * a_ref[halo:, :]
    for j in range(CONV_WIDTH - 1):
        back = CONV_WIDTH - 1 - j
        a = a + cw[j:j + 1, :] * a_ref[pl.ds(halo - back, bm), :]
    o_ref[...] = (_silu(a) * u).astype(o_ref.dtype)


def _ffn_up(h, w_up, w_gate, conv_w, conv_b, layer):
    R, K = h.shape
    N = w_up.shape[2]
    bm = _pick(R, (1408, 768, 512, 384, 256, 128))
    bn = _pick(N, (512, 256, 128))
    halo = BF16_ROWS
    hb = bm // halo
    return pl.pallas_call(
        _ffn_up_body,
        grid=(R // bm, N // bn),
        in_specs=[pl.BlockSpec((bm, K), lambda i, j: (i, 0)),
                  pl.BlockSpec((halo, K), lambda i, j: (jnp.maximum(i * hb - 1, 0), 0)),
                  pl.BlockSpec((None, K, bn), lambda i, j: (layer, 0, j)),
                  pl.BlockSpec((None, K, bn), lambda i, j: (layer, 0, j)),
                  pl.BlockSpec((None, CONV_WIDTH, bn), lambda i, j: (layer, 0, j)),
                  pl.BlockSpec((None, 1, bn), lambda i, j: (layer, 0, j))],
        out_specs=pl.BlockSpec((bm, bn), lambda i, j: (i, j)),
        out_shape=jax.ShapeDtypeStruct((R, N), BF16),
        scratch_shapes=[pltpu.VMEM((bm + halo, K), BF16),
                        pltpu.VMEM((bm + halo, bn), F32)],
        compiler_params=_params(("parallel", "arbitrary")),
        name="ffn_up",
    )(h, h, w_up, w_gate, conv_w, conv_b.reshape(conv_b.shape[0], 1, N))


def _swa_body(sink_ref, q_ref, kp_ref, kc_ref, vp_ref, vc_ref, o_ref):
    n = pl.program_id(1)
    blk = q_ref.shape[0]
    dh = SWA_HEAD_DIM
    group = SWA_HEADS // SWA_KV_HEADS
    qi = lax.broadcasted_iota(jnp.int32, (blk, 2 * blk), 0)
    kj = lax.broadcasted_iota(jnp.int32, (blk, 2 * blk), 1)
    rel = qi + blk - kj
    key_pos = n * blk + kj - blk
    mask = (rel >= 0) & (rel < blk) & (key_pos >= FRAME_PAD)
    for g in range(SWA_KV_HEADS):
        ks = slice(g * dh, (g + 1) * dh)
        k = jnp.concatenate([kp_ref[:, ks], kc_ref[:, ks]], axis=0)
        v = jnp.concatenate([vp_ref[:, ks], vc_ref[:, ks]], axis=0)
        for hh in range(group):
            h = g * group + hh
            hs = slice(h * dh, (h + 1) * dh)
            q = q_ref[:, hs] * (dh ** -0.5)
            s = lax.dot_general(q, k, (((1,), (1,)), ((), ())), preferred_element_type=F32)
            s = jnp.where(mask, s, NEG)
            sink = sink_ref[h]
            m = jnp.maximum(jnp.max(s, axis=-1, keepdims=True), sink)
            p = jnp.exp(s - m)
            den = jnp.sum(p, axis=-1, keepdims=True) + jnp.exp(sink - m)
            o = jnp.dot(p.astype(BF16), v, preferred_element_type=F32) / den
            o_ref[:, hs] = o.astype(o_ref.dtype)


def _swa_attention(z, sinks, batch, n_blocks):
    R = z.shape[0]
    blk = ATT_BLOCK
    kcol = COL_KA // SWA_KV
    vcol = COL_VA // SWA_KV

    def cur(b, n):
        return b * n_blocks + n

    def prev(b, n):
        return b * n_blocks + jnp.maximum(n - 1, 0)

    return pl.pallas_call(
        _swa_body,
        grid=(batch, n_blocks),
        in_specs=[pl.BlockSpec(memory_space=pltpu.SMEM),
                  pl.BlockSpec((blk, SWA_Q), lambda b, n: (cur(b, n), 0)),
                  pl.BlockSpec((blk, SWA_KV), lambda b, n: (prev(b, n), kcol)),
                  pl.BlockSpec((blk, SWA_KV), lambda b, n: (cur(b, n), kcol)),
                  pl.BlockSpec((blk, SWA_KV), lambda b, n: (prev(b, n), vcol)),
                  pl.BlockSpec((blk, SWA_KV), lambda b, n: (cur(b, n), vcol))],
        out_specs=pl.BlockSpec((blk, SWA_Q), lambda b, n: (cur(b, n), 0)),
        out_shape=jax.ShapeDtypeStruct((R, SWA_Q), BF16),
        compiler_params=_params(("parallel", "arbitrary")),
        name="swa_attention",
    )(sinks, z, z, z, z, z)


def _mla_q_body(cq_ref, g_ref, w_ref, wrot_ref, cos_ref, sin_ref, bias_ref, o_ref):
    x = cq_ref[...].astype(F32)
    ms = jnp.mean(x * x, axis=-1, keepdims=True)
    h = (x * lax.rsqrt(ms + EPS) * g_ref[...]).astype(BF16)
    a = jnp.dot(h, w_ref[...], preferred_element_type=F32)
    ar = jnp.dot(h, wrot_ref[...], preferred_element_type=F32)
    c = cos_ref[...]
    s = sin_ref[...]
    bias = bias_ref[...]
    for hh in range(MLA_HEADS):
        sl = slice(hh * LANES, (hh + 1) * LANES)
        o_ref[:, sl] = (a[:, sl] * c + ar[:, sl] * s + bias).astype(o_ref.dtype)


def _mla_q_proj(z, gain, w, wrot, cos_t, sin_t, bias_t, t_blocks, bm):
    R = z.shape[0]
    N = w.shape[1]
    table = pl.BlockSpec((bm, LANES), lambda i: (i % t_blocks, 0))
    return pl.pallas_call(
        _mla_q_body,
        grid=(R // bm,),
        in_specs=[pl.BlockSpec((bm, MLA_Q_RANK), lambda i: (i, COL_CQ // MLA_Q_RANK)),
                  pl.BlockSpec((1, MLA_Q_RANK), lambda i: (0, 0)),
                  pl.BlockSpec((MLA_Q_RANK, N), lambda i: (0, 0)),
                  pl.BlockSpec((MLA_Q_RANK, N), lambda i: (0, 0)),
                  table, table, table],
        out_specs=pl.BlockSpec((bm, N), lambda i: (i, 0)),
        out_shape=jax.ShapeDtypeStruct((R, N), BF16),
        compiler_params=_params(("parallel",)),
        name="mla_q_proj",
    )(z, gain.reshape(1, -1), w, wrot, cos_t, sin_t, bias_t)


def _mla_kv_body(ckv_ref, kr_ref, krrot_ref, g_ref, wk_ref, wv_ref, cos_ref, sin_ref, bias_ref,
                 k_ref, v_ref):
    x = ckv_ref[...].astype(F32)
    ms = jnp.mean(x * x, axis=-1, keepdims=True)
    h = (x * lax.rsqrt(ms + EPS) * g_ref[...]).astype(BF16)
    ak = jnp.dot(h, wk_ref[...], preferred_element_type=F32)
    v_ref[...] = jnp.dot(h, wv_ref[...], preferred_element_type=F32).astype(v_ref.dtype)
    rope = (kr_ref[...].astype(F32) * cos_ref[...] + krrot_ref[...].astype(F32) * sin_ref[...]
            + bias_ref[...])
    for hh in range(MLA_HEADS):
        sl = slice(hh * LANES, (hh + 1) * LANES)
        k_ref[:, sl] = (ak[:, sl] + rope).astype(k_ref.dtype)


def _mla_kv_proj(z, gain, wk, wv, cos_t, sin_t, bias_t, t_blocks, bm):
    R = z.shape[0]
    table = pl.BlockSpec((bm, LANES), lambda i: (i % t_blocks, 0))
    return pl.pallas_call(
        _mla_kv_body,
        grid=(R // bm,),
        in_specs=[pl.BlockSpec((bm, MLA_KV_RANK), lambda i: (i, COL_CKV // MLA_KV_RANK)),
                  pl.BlockSpec((bm, LANES), lambda i: (i, COL_KR // LANES)),
                  pl.BlockSpec((bm, LANES), lambda i: (i, COL_KRROT // LANES)),
                  pl.BlockSpec((1, MLA_KV_RANK), lambda i: (0, 0)),
                  pl.BlockSpec(wk.shape, lambda i: (0, 0)),
                  pl.BlockSpec(wv.shape, lambda i: (0, 0)),
                  table, table, table],
        out_specs=[pl.BlockSpec((bm, wk.shape[1]), lambda i: (i, 0)),
                   pl.BlockSpec((bm, wv.shape[1]), lambda i: (i, 0))],
        out_shape=[jax.ShapeDtypeStruct((R, wk.shape[1]), BF16),
                   jax.ShapeDtypeStruct((R, wv.shape[1]), BF16)],
        compiler_params=_params(("parallel",)),
        name="mla_kv_proj",
    )(z, z, z, gain.reshape(1, -1), wk, wv, cos_t, sin_t, bias_t)


def _lane_blocks(s):
    return [s[:, c * LANES:(c + 1) * LANES] for c in range(s.shape[1] // LANES)]


def _mla_attn_body(q_ref, k_ref, v_ref, o_ref, vext_ref, m_ref, acc_ref, *, bq, heads):
    i = pl.program_id(2)
    dv = MLA_V_DIM
    kb = MXU_COLS
    T = k_ref.shape[0]
    lane = lax.broadcasted_iota(jnp.int32, (1, LANES), 1)
    own = [(lane >= hh * dv) & (lane < (hh + 1) * dv) for hh in range(heads)]

    @pl.when(i == 0)
    def _():
        for c in range(T // bq):
            rows = slice(c * bq, (c + 1) * bq)
            vv = v_ref[rows, :]
            for hh in range(heads):
                vext_ref[hh, rows, :] = jnp.where(own[hh], vv, jnp.ones_like(vv))

    def scores(hh, start, size, row0=0):
        sl = slice(hh * LANES, (hh + 1) * LANES)
        return lax.dot_general(q_ref[row0:, sl], k_ref[pl.ds(start, size), sl],
                               (((1,), (1,)), ((), ())), preferred_element_type=F32)

    def online(hh, start, size, row0=0, mask=None):
        s = scores(hh, start, size, row0)
        if mask is not None:
            s = mask(s)
        blocks = _lane_blocks(s)
        top = blocks[0]
        for blk in blocks[1:]:
            top = jnp.maximum(top, blk)
        m = m_ref[hh, row0:, :]
        m_new = jnp.maximum(m, jnp.max(top, axis=-1, keepdims=True))
        p = jnp.concatenate([jnp.exp2(blk - m_new) for blk in blocks], axis=1).astype(BF16)
        m_ref[hh, row0:, :] = m_new
        acc_ref[hh, row0:, :] = (jnp.exp2(m - m_new) * acc_ref[hh, row0:, :]
                                 + jnp.dot(p, vext_ref[hh, pl.ds(start, size), :],
                                           preferred_element_type=F32))

    m_ref[...] = jnp.full(m_ref.shape, NEG, F32)
    acc_ref[...] = jnp.zeros(acc_ref.shape, F32)

    visible = i * bq
    big = 2 * kb
    n_big = visible // big
    left = visible - n_big * big

    def big_tile(j, carry):
        for hh in range(heads):
            online(hh, pl.multiple_of(j * big, big), big)
        return carry

    lax.fori_loop(0, n_big, big_tile, 0)

    @pl.when(left >= kb)
    def _():
        for hh in range(heads):
            online(hh, pl.multiple_of(n_big * big, kb), kb)

    @pl.when(left % kb > 0)
    def _():
        for hh in range(heads):
            online(hh, pl.multiple_of(visible - LANES, LANES), LANES)

    for k0 in range(0, bq, big):
        size = min(big, bq - k0)
        lower = (lax.broadcasted_iota(jnp.int32, (size, 1), 0)
                 >= lax.broadcasted_iota(jnp.int32, (1, size), 1))

        def causal(s, size=size, lower=lower):
            head = jnp.where(lower, s[:size], NEG)
            return head if s.shape[0] == size else jnp.concatenate([head, s[size:]], axis=0)

        for hh in range(heads):
            online(hh, pl.multiple_of(visible + k0, LANES), size, row0=k0, mask=causal)

    out = jnp.zeros((bq, LANES), F32)
    for hh in range(heads):
        acc = acc_ref[hh]
        denom = pltpu.roll(acc, dv, axis=1)
        out = jnp.where(own[hh], acc / denom, out)
    q_pos = i * bq + lax.broadcasted_iota(jnp.int32, (bq, 1), 0)
    o_ref[...] = jnp.where(q_pos >= FRAME_PAD, out, 0.0).astype(o_ref.dtype)


def _mla_attention(q, k, v, batch, T, bq):
    R = q.shape[0]
    heads = LANES // MLA_V_DIM
    groups = MLA_HEADS // heads
    t_blocks = T // bq
    wq = heads * LANES
    return pl.pallas_call(
        functools.partial(_mla_attn_body, bq=bq, heads=heads),
        grid=(batch, groups, t_blocks),
        in_specs=[pl.BlockSpec((bq, wq), lambda b, g, i: (b * t_blocks + i, g)),
                  pl.BlockSpec((T, wq), lambda b, g, i: (b, g)),
                  pl.BlockSpec((T, LANES), lambda b, g, i: (b, g))],
        out_specs=pl.BlockSpec((bq, LANES), lambda b, g, i: (b * t_blocks + i, g)),
        out_shape=jax.ShapeDtypeStruct((R, MLA_HEADS * MLA_V_DIM), BF16),
        scratch_shapes=[pltpu.VMEM((heads, T, LANES), BF16),
                        pltpu.VMEM((heads, bq, LANES), F32),
                        pltpu.VMEM((heads, bq, LANES), F32)],
        compiler_params=_params(("parallel", "parallel", "arbitrary")),
        name="mla_attention",
    )(q, k, v)


def _rec_in_body(x_ref, wq_ref, wf_ref, wi_ref, wg_ref, la_ref, lc_ref,
                 q_ref, lf_ref, i_ref, g_ref):
    x = x_ref[...]
    q = jnp.dot(x, wq_ref[...], preferred_element_type=F32)
    q_ref[...] = _silu(q).astype(q_ref.dtype)
    f = jnp.dot(x, wf_ref[...], preferred_element_type=F32) * LOG2_E
    log_sig = jnp.minimum(f, 0.0) - jnp.log2(1.0 + jnp.exp2(-jnp.abs(f)))
    y = lc_ref[...] + log_sig
    la = la_ref[...]
    gap = jnp.minimum(la - y, y - la)
    lf_ref[...] = jnp.maximum(la, y) + jnp.log2(1.0 + jnp.exp2(gap))
    i_ref[...] = jnp.dot(x, wi_ref[...], preferred_element_type=F32).astype(i_ref.dtype)
    g = jnp.dot(x, wg_ref[...], preferred_element_type=F32)
    g_ref[...] = _silu(g).astype(g_ref.dtype)


def _rec_in_proj(h, w_in, layer, log_lb, log_1m_lb):
    R, K = h.shape
    D = w_in.shape[2] // 4
    bm = _pick(R, (768, 512, 384, 256, 128))
    bn = _pick(D, (512, 256, 128))
    nb = D // bn

    def wspec(sec):
        return pl.BlockSpec((None, K, bn), lambda i, j: (layer, 0, sec * nb + j))

    ospec = pl.BlockSpec((bm, bn), lambda i, j: (i, j))
    vspec = pl.BlockSpec((1, bn), lambda i, j: (0, j))
    return pl.pallas_call(
        _rec_in_body,
        grid=(R // bm, nb),
        in_specs=[pl.BlockSpec((bm, K), lambda i, j: (i, 0)),
                  wspec(0), wspec(1), wspec(2), wspec(3), vspec, vspec],
        out_specs=[ospec, ospec, ospec, ospec],
        out_shape=[jax.ShapeDtypeStruct((R, D), BF16),
                   jax.ShapeDtypeStruct((R, D), F32),
                   jax.ShapeDtypeStruct((R, D), BF16),
                   jax.ShapeDtypeStruct((R, D), BF16)],
        compiler_params=_params(("parallel", "arbitrary")),
        name="rec_in_proj",
    )(h, w_in, w_in, w_in, w_in, log_lb.reshape(1, D), log_1m_lb.reshape(1, D))


def _segment_tail(cum, w):
    C, W = cum.shape
    if 2 * w >= F32_ROWS * 2:
        pieces = [jnp.broadcast_to(cum[r:r + 1, :], (2 * w, W))
                  for r in range(w - 1, C, 2 * w)]
        return pieces[0] if len(pieces) == 1 else jnp.concatenate(pieces, axis=0)
    tiles = cum.reshape(C // F32_ROWS, F32_ROWS, W)
    sub = lax.broadcasted_iota(jnp.int32, (1, F32_ROWS, 1), 1)
    out = None
    for r in reversed(range(w - 1, F32_ROWS, 2 * w)):
        piece = jnp.broadcast_to(tiles[:, r:r + 1, :], tiles.shape)
        out = piece if out is None else jnp.where(sub < r + w + 1, piece, out)
    return out.reshape(C, W)


def _rec_body(q_ref, lf_ref, v_ref, gs_ref, gain_ref, o_ref, state_ref):
    c = pl.program_id(1)
    C = q_ref.shape[0]
    D = REC_DIM
    contract_last = (((1,), (1,)), ((), ()))
    contract_first = (((0,), (0,)), ((), ()))

    @pl.when(c == 0)
    def _():
        state_ref[...] = jnp.zeros_like(state_ref)

    row = lax.broadcasted_iota(jnp.int32, (C, 1), 0)
    g = jnp.where(c * C + row >= FRAME_PAD, lf_ref[...], 0.0)
    kk = 1.0 - jnp.exp2(g)
    q16 = q_ref[...]
    q = q16.astype(F32)
    k16 = kk.astype(BF16)

    ti = lax.broadcasted_iota(jnp.int32, (C, C), 0)
    si = lax.broadcasted_iota(jnp.int32, (C, C), 1)
    bit_diff = jnp.where(ti > si, ti ^ si, 0)
    cum = g
    levels = []
    w = 1
    while w < C:
        upper = (row & w) != 0
        tail = _segment_tail(cum, w)
        f = jnp.exp2(jnp.where(upper, cum, tail - cum))
        pair_mask = (bit_diff >= w) & (bit_diff < 2 * w)
        levels.append(((jnp.where(upper, q, kk) * f).astype(BF16), pair_mask))
        cum = cum + jnp.where(upper, tail, 0.0)
        w *= 2
    b = cum
    same_row = ti == si

    b_last = b[C - 1:C, :]
    q_inter = (q * jnp.exp2(b)).astype(BF16)
    k_state = (kk * jnp.exp2(b_last - b)).astype(BF16)
    state_decay = jnp.exp2(b_last)

    v = v_ref[...]
    gain = gain_ref[...]
    for h in range(REC_HEADS):
        sl = slice(h * D, (h + 1) * D)
        st = state_ref[h]
        att = jnp.where(same_row,
                        lax.dot_general(q16[:, sl], k16[:, sl], contract_last,
                                        preferred_element_type=F32), 0.0)
        for qk_lvl, pair_mask in levels:
            att = jnp.where(pair_mask,
                            lax.dot_general(qk_lvl[:, sl], qk_lvl[:, sl], contract_last,
                                            preferred_element_type=F32), att)
        o = (lax.dot_general(q_inter[:, sl], st.astype(BF16), contract_last,
                             preferred_element_type=F32)
             + jnp.dot(att.astype(BF16), v[:, sl], preferred_element_type=F32))
        upd = lax.dot_general(v[:, sl], k_state[:, sl], contract_first,
                              preferred_element_type=F32)
        state_ref[h] = state_decay[:, sl] * st + upd
        ms = jnp.mean(o * o, axis=-1, keepdims=True)
        y = o * lax.rsqrt(ms + EPS) * gain * gs_ref[:, sl].astype(F32)
        o_ref[:, sl] = y.astype(o_ref.dtype)


def _rec_mixer(qs, log_f, v, gs, out_gain, batch, T):
    R, W = qs.shape
    C = REC_CHUNK
    n_chunks = T // C
    spec = pl.BlockSpec((C, W), lambda b, c: (b * n_chunks + c, 0))
    return pl.pallas_call(
        _rec_body,
        grid=(batch, n_chunks),
        in_specs=[spec, spec, spec, spec, pl.BlockSpec((1, REC_DIM), lambda b, c: (0, 0))],
        out_specs=spec,
        out_shape=jax.ShapeDtypeStruct((R, W), BF16),
        scratch_shapes=[pltpu.VMEM((REC_HEADS, REC_DIM, REC_DIM), F32)],
        compiler_params=_params(("parallel", "arbitrary")),
        name="rec_mixer",
    )(qs, log_f, v, gs, out_gain.reshape(1, REC_DIM))


def _att_in_weight(w_in):
    d = w_in.shape[0]
    half = MLA_ROPE_DIM // 2
    w_kr = w_in[:, COL_KR:COL_KR + MLA_ROPE_DIM]
    rot = jnp.concatenate([-w_kr[:, half:], w_kr[:, :half]], axis=1)
    left = jnp.zeros((d, MLA_NOPE_DIM), w_in.dtype)
    right = jnp.zeros((d, LANES - MLA_NOPE_DIM - MLA_ROPE_DIM), w_in.dtype)
    return jnp.concatenate([w_in[:, :COL_KR], left, w_kr, right, left, rot, right], axis=1)


def _mla_q_weights(w_uq):
    r = w_uq.shape[0]
    half = MLA_ROPE_DIM // 2
    w = w_uq.reshape(r, MLA_HEADS, MLA_NOPE_DIM + MLA_ROPE_DIM)
    nope, rope = w[..., :MLA_NOPE_DIM], w[..., MLA_NOPE_DIM:]
    rot = jnp.concatenate([-rope[..., half:], rope[..., :half]], axis=-1)
    fill = jnp.zeros((r, MLA_HEADS, LANES - MLA_NOPE_DIM - MLA_ROPE_DIM), w.dtype)
    plain = jnp.concatenate([nope, rope, fill], axis=-1).reshape(r, MLA_HEADS * LANES)
    rotated = jnp.concatenate([jnp.zeros_like(nope), rot, fill], axis=-1).reshape(r, MLA_HEADS * LANES)
    return plain, rotated


def _mla_kv_weights(w_ukv):
    r = w_ukv.shape[0]
    w = w_ukv.reshape(r, MLA_HEADS, MLA_NOPE_DIM + MLA_V_DIM)
    k_nope, v = w[..., :MLA_NOPE_DIM], w[..., MLA_NOPE_DIM:]
    wk = jnp.concatenate([k_nope, jnp.zeros((r, MLA_HEADS, LANES - MLA_NOPE_DIM), w.dtype)], axis=-1)
    return wk.reshape(r, MLA_HEADS * LANES), v.reshape(r, MLA_HEADS * MLA_V_DIM)


def _rope_tables(T):
    half = MLA_ROPE_DIM // 2
    inv_freq = ROPE_THETA ** (-2.0 * jnp.arange(half, dtype=F32) / MLA_ROPE_DIM)
    pos = jnp.maximum(jnp.arange(T) - FRAME_PAD, 0).astype(F32)
    ang = pos[:, None] * inv_freq[None, :]
    cos, sin = jnp.cos(ang), jnp.sin(ang)
    left = jnp.ones((T, MLA_NOPE_DIM), F32)
    right = jnp.zeros((T, LANES - MLA_NOPE_DIM - MLA_ROPE_DIM), F32)
    cos_t = jnp.concatenate([left, cos, cos, right], axis=1)
    sin_t = jnp.concatenate([0.0 * left, sin, sin, right], axis=1)
    spare = (jnp.arange(LANES) == MLA_NOPE_DIM + MLA_ROPE_DIM).astype(F32)[None, :]
    is_pad = (jnp.arange(T) < FRAME_PAD).astype(F32)[:, None]
    q_bias = jnp.broadcast_to(spare, (T, LANES))
    k_bias = NEG * is_pad * spare
    return cos_t, sin_t, q_bias, k_bias


def kernel(x, meta_tokens, mix_norm, ffn_norm, final_norm, att_w_in, att_sinks, mla_q_norm,
           mla_w_uq, mla_kv_norm, mla_w_ukv, att_w_out, rec_w_in, rec_lower_bounds, rec_out_norm,
           rec_w_out, ffn_w_up, ffn_w_gate, ffn_conv_w, ffn_conv_b, ffn_w_down):
    B, S, D = x.shape
    depth = mix_norm.shape[0]
    T = FRAME_PAD + N_META + S
    assert T % ATT_BLOCK == 0 and T % REC_CHUNK == 0
    R = B * T

    meta = jnp.broadcast_to(meta_tokens[None].astype(x.dtype), (B, N_META, D))
    hs = jnp.concatenate([jnp.zeros((B, FRAME_PAD, D), x.dtype), meta, x], axis=1).reshape(R, D)

    cos_t, sin_t, q_bias, k_bias = _rope_tables(T)
    q_scale = (MLA_NOPE_DIM + MLA_ROPE_DIM) ** -0.5 * LOG2_E
    sm = jax.nn.softmax(rec_lower_bounds.astype(F32), axis=0)
    lower = jnp.cumsum(sm.at[0].set(0.0), axis=0)
    log_lb, log_1m_lb = jnp.log(lower) * LOG2_E, jnp.log1p(-lower) * LOG2_E

    bq = _pick(T, (1408, 384, 128))
    t_blocks = T // bq

    att_w_out16, rec_w_out16 = att_w_out.astype(BF16), rec_w_out.astype(BF16)
    rec_w_in16, ffn_w_down16 = rec_w_in.astype(BF16), ffn_w_down.astype(BF16)

    h = _rmsnorm(hs, mix_norm[0], BF16)
    for layer in range(depth):
        idx = layer // 2
        if layer % 2 == 0:
            z = _matmul(h, _att_in_weight(att_w_in[idx]).astype(BF16), BF16)
            o_a = _swa_attention(z, att_sinks[idx].astype(F32), B, T // ATT_BLOCK)
            wq, wq_rot = _mla_q_weights(mla_w_uq[idx])
            wk, wv = _mla_kv_weights(mla_w_ukv[idx])
            q = _mla_q_proj(z, mla_q_norm[idx], wq.astype(BF16), wq_rot.astype(BF16),
                            cos_t * q_scale, sin_t * q_scale, q_bias, t_blocks, bq)
            k, v = _mla_kv_proj(z, mla_kv_norm[idx], wk.astype(BF16), wv.astype(BF16),
                                cos_t, sin_t, k_bias, t_blocks, bq)
            o_b = _mla_attention(q, k, v, B, T, _pick(T, (1408, 384, 128)))
            mixed = jnp.concatenate([o_a, o_b], axis=1)
            w_out = att_w_out16
        else:
            qs, log_f, iv, gs = _rec_in_proj(h, rec_w_in16, idx, log_lb[idx], log_1m_lb[idx])
            mixed = _rec_mixer(qs, log_f, iv, gs, rec_out_norm[idx], B, T)
            w_out = rec_w_out16
        hs, h = _proj_residual_norm(mixed, w_out, idx, hs, ffn_norm[layer], BF16)
        g = _ffn_up(h, ffn_w_up, ffn_w_gate, ffn_conv_w, ffn_conv_b, layer)
        if layer + 1 < depth:
            hs, h = _proj_residual_norm(g, ffn_w_down16, layer, hs, mix_norm[layer + 1], BF16)
        else:
            out = _final_proj_norm(g, ffn_w_down16, layer, hs, final_norm, B, T,
                                   FRAME_PAD + N_META, x.dtype)
    return out.reshape(B, S, D)
```

```python
import functools

import jax
import jax.numpy as jnp
from jax import lax
from jax.experimental import pallas as pl
from jax.experimental.pallas import tpu as pltpu

F32 = jnp.float32
BF16 = jnp.bfloat16

N_META = 16
EPS = 1e-6
NEG = -1e30
LOG2_E = 1.4426950408889634
ATT_BLOCK = 128
SWA_HEADS = 16
SWA_KV_HEADS = 4
SWA_HEAD_DIM = 64
MLA_HEADS = 16
MLA_Q_RANK = 512
MLA_KV_RANK = 256
MLA_NOPE_DIM = 64
MLA_ROPE_DIM = 32
MLA_V_DIM = 64
ROPE_THETA = 10000.0
REC_HEADS = 16
REC_DIM = 128
CONV_WIDTH = 3

LANES = 128
MXU_COLS = 256
F32_ROWS = 8
BF16_ROWS = 16
FRAME_PAD = (-N_META) % ATT_BLOCK
REC_CHUNK = 128
VMEM_LIMIT = 56 * 1024 * 1024

SWA_Q = SWA_HEADS * SWA_HEAD_DIM
SWA_KV = SWA_KV_HEADS * SWA_HEAD_DIM
COL_QA = 0
COL_KA = SWA_Q
COL_VA = SWA_Q + SWA_KV
COL_CQ = SWA_Q + 2 * SWA_KV
COL_CKV = COL_CQ + MLA_Q_RANK
COL_KR = COL_CKV + MLA_KV_RANK
COL_KRROT = COL_KR + LANES
ATT_IN_PAD = COL_KRROT + LANES


def _pick(n, candidates):
    for c in candidates:
        if n % c == 0:
            return c
    raise ValueError(f"no block size in {candidates} divides {n}")


def _params(sem):
    return pltpu.CompilerParams(dimension_semantics=sem, vmem_limit_bytes=VMEM_LIMIT)


def _silu(x):
    return x * (1.0 / (1.0 + jnp.exp(-x)))


def _rmsnorm_body(x_ref, g_ref, o_ref):
    x = x_ref[...]
    ms = jnp.mean(x * x, axis=-1, keepdims=True)
    o_ref[...] = (x * lax.rsqrt(ms + EPS) * g_ref[...]).astype(o_ref.dtype)


def _rmsnorm(x, gain, out_dtype):
    R, D = x.shape
    bm = _pick(R, (768, 512, 384, 256, 128))
    return pl.pallas_call(
        _rmsnorm_body,
        grid=(R // bm,),
        in_specs=[pl.BlockSpec((bm, D), lambda i: (i, 0)),
                  pl.BlockSpec((1, D), lambda i: (0, 0))],
        out_specs=pl.BlockSpec((bm, D), lambda i: (i, 0)),
        out_shape=jax.ShapeDtypeStruct((R, D), out_dtype),
        compiler_params=_params(("parallel",)),
        name="rmsnorm",
    )(x, gain.reshape(1, D))


def _matmul_body(x_ref, w_ref, o_ref):
    o_ref[...] = jnp.dot(x_ref[...], w_ref[...], preferred_element_type=F32).astype(o_ref.dtype)


def _matmul(x, w, out_dtype):
    R, K = x.shape
    N = w.shape[1]
    bm = _pick(R, (1408, 768, 512, 384, 256, 128))
    bn = _pick(N, (1280, 1024, 512, 256, 128))
    return pl.pallas_call(
        _matmul_body,
        grid=(R // bm, N // bn),
        in_specs=[pl.BlockSpec((bm, K), lambda i, j: (i, 0)),
                  pl.BlockSpec((K, bn), lambda i, j: (0, j))],
        out_specs=pl.BlockSpec((bm, bn), lambda i, j: (i, j)),
        out_shape=jax.ShapeDtypeStruct((R, N), out_dtype),
        compiler_params=_params(("parallel", "arbitrary")),
        name="att_in_proj",
    )(x, w)


def _proj_res_body(*refs, n_x=1):
    x_refs, (w_ref, res_ref, g_ref, *out_refs) = refs[:n_x], refs[n_x:]
    *hs_refs, h_ref = out_refs
    y = res_ref[...]
    k0 = 0
    for x_ref in x_refs:
        k1 = k0 + x_ref.shape[1]
        y = y + jnp.dot(x_ref[...], w_ref[k0:k1, :], preferred_element_type=F32)
        k0 = k1
    for hs_ref in hs_refs:
        hs_ref[...] = y
    ms = jnp.mean(y * y, axis=-1, keepdims=True)
    h_ref[...] = (y * lax.rsqrt(ms + EPS) * g_ref[...]).astype(h_ref.dtype)


def _proj_residual_norm(x, w_stack, layer, res, gain, h_dtype, emit_residual=True):
    xs = x if isinstance(x, tuple) else (x,)
    R = xs[0].shape[0]
    K, N = w_stack.shape[1:]
    assert sum(part.shape[1] for part in xs) == K
    bm = _pick(R, (384, 256, 128))
    row_spec = pl.BlockSpec((bm, N), lambda i: (i, 0))
    out_specs = [row_spec]
    out_shape = [jax.ShapeDtypeStruct((R, N), h_dtype)]
    if emit_residual:
        out_specs = [row_spec] + out_specs
        out_shape = [jax.ShapeDtypeStruct((R, N), F32)] + out_shape
    return pl.pallas_call(
        functools.partial(_proj_res_body, n_x=len(xs)),
        grid=(R // bm,),
        in_specs=[pl.BlockSpec((bm, part.shape[1]), lambda i: (i, 0)) for part in xs]
                 + [pl.BlockSpec((None, K, N), lambda i: (layer, 0, 0),
                                 pipeline_mode=pl.Buffered(1)),
                    row_spec,
                    pl.BlockSpec((1, N), lambda i: (0, 0))],
        out_specs=out_specs,
        out_shape=out_shape,
        compiler_params=_params(("parallel",)),
        name="proj_residual_norm",
    )(*xs, w_stack, res, gain.reshape(1, N))


def _final_proj_norm(x, w_stack, layer, res, gain, batch, T, skip, out_dtype):
    K = x.shape[1]
    N = w_stack.shape[2]
    S = T - skip
    bm = _pick(S, (256, 128))
    nb = S // bm

    def frame_row(b, j):
        return pl.multiple_of(b * T + skip + j * bm, LANES)

    return pl.pallas_call(
        _proj_res_body,
        grid=(batch, nb),
        in_specs=[pl.BlockSpec((pl.Element(bm), pl.Element(K)), lambda b, j: (frame_row(b, j), 0)),
                  pl.BlockSpec((None, K, N), lambda b, j: (layer, 0, 0),
                               pipeline_mode=pl.Buffered(1)),
                  pl.BlockSpec((pl.Element(bm), pl.Element(N)), lambda b, j: (frame_row(b, j), 0)),
                  pl.BlockSpec((1, N), lambda b, j: (0, 0))],
        out_specs=[pl.BlockSpec((bm, N), lambda b, j: (b * nb + j, 0))],
        out_shape=[jax.ShapeDtypeStruct((batch * S, N), out_dtype)],
        compiler_params=_params(("parallel", "parallel")),
        name="final_proj_norm",
    )(x, w_stack, res, gain.reshape(1, N))[0]


def _ffn_up_body(x_ref, halo_ref, wu_ref, wg_ref, cw_ref, cb_ref, o_ref, lhs_ref, a_ref):
    bm = x_ref.shape[0]
    halo = halo_ref.shape[0]

    @pl.when(pl.program_id(1) == 0)
    def _():
        lhs_ref[0:halo, :] = halo_ref[...]
        lhs_ref[halo:, :] = x_ref[...]

    a_ref[...] = jnp.dot(lhs_ref[...], wg_ref[...].astype(BF16), preferred_element_type=F32)
    u = jnp.dot(x_ref[...], wu_ref[...].astype(BF16), preferred_element_type=F32)
    cw = cw_ref[...]
    a = cb_ref[...] + cw[CONV_WIDTH - 1:CONV_WIDTH, :] * a_ref[halo:, :]
    for j in range(CONV_WIDTH - 1):
        back = CONV_WIDTH - 1 - j
        a = a + cw[j:j + 1, :] * a_ref[pl.ds(halo - back, bm), :]
    o_ref[...] = (_silu(a) * u).astype(o_ref.dtype)


def _ffn_up(h, w_up, w_gate, conv_w, conv_b, layer):
    R, K = h.shape
    N = w_up.shape[2]
    bm = _pick(R, (1408, 768, 512, 384, 256, 128))
    bn = _pick(N, (512, 256, 128))
    halo = BF16_ROWS
    hb = bm // halo
    return pl.pallas_call(
        _ffn_up_body,
        grid=(R // bm, N // bn),
        in_specs=[pl.BlockSpec((bm, K), lambda i, j: (i, 0)),
                  pl.BlockSpec((halo, K), lambda i, j: (jnp.maximum(i * hb - 1, 0), 0)),
                  pl.BlockSpec((None, K, bn), lambda i, j: (layer, 0, j)),
                  pl.BlockSpec((None, K, bn), lambda i, j: (layer, 0, j)),
                  pl.BlockSpec((None, CONV_WIDTH, bn), lambda i, j: (layer, 0, j)),
                  pl.BlockSpec((None, 1, bn), lambda i, j: (layer, 0, j))],
        out_specs=pl.BlockSpec((bm, bn), lambda i, j: (i, j)),
        out_shape=jax.ShapeDtypeStruct((R, N), BF16),
        scratch_shapes=[pltpu.VMEM((bm + halo, K), BF16),
                        pltpu.VMEM((bm + halo, bn), F32)],
        compiler_params=_params(("parallel", "arbitrary")),
        name="ffn_up",
    )(h, h, w_up, w_gate, conv_w, conv_b.reshape(conv_b.shape[0], 1, N))


def _swa_body(sink_ref, q_ref, kp_ref, kc_ref, vp_ref, vc_ref, o_ref):
    n = pl.program_id(1)
    blk = q_ref.shape[0]
    dh = SWA_HEAD_DIM
    group = SWA_HEADS // SWA_KV_HEADS
    qi = lax.broadcasted_iota(jnp.int32, (blk, 2 * blk), 0)
    kj = lax.broadcasted_iota(jnp.int32, (blk, 2 * blk), 1)
    rel = qi + blk - kj
    key_pos = n * blk + kj - blk
    mask = (rel >= 0) & (rel < blk) & (key_pos >= FRAME_PAD)
    for g in range(SWA_KV_HEADS):
        ks = slice(g * dh, (g + 1) * dh)
        k = jnp.concatenate([kp_ref[:, ks], kc_ref[:, ks]], axis=0)
        v = jnp.concatenate([vp_ref[:, ks], vc_ref[:, ks]], axis=0)
        for hh in range(group):
            h = g * group + hh
            hs = slice(h * dh, (h + 1) * dh)
            q = q_ref[:, hs] * (dh ** -0.5)
            s = lax.dot_general(q, k, (((1,), (1,)), ((), ())), preferred_element_type=F32)
            s = jnp.where(mask, s, NEG)
            sink = sink_ref[h]
            m = jnp.maximum(jnp.max(s, axis=-1, keepdims=True), sink)
            p = jnp.exp(s - m)
            den = jnp.sum(p, axis=-1, keepdims=True) + jnp.exp(sink - m)
            o = jnp.dot(p.astype(BF16), v, preferred_element_type=F32) / den
            o_ref[:, hs] = o.astype(o_ref.dtype)


def _swa_attention(z, sinks, batch, n_blocks):
    R = z.shape[0]
    blk = ATT_BLOCK
    kcol = COL_KA // SWA_KV
    vcol = COL_VA // SWA_KV

    def cur(b, n):
        return b * n_blocks + n

    def prev(b, n):
        return b * n_blocks + jnp.maximum(n - 1, 0)

    return pl.pallas_call(
        _swa_body,
        grid=(batch, n_blocks),
        in_specs=[pl.BlockSpec(memory_space=pltpu.SMEM),
                  pl.BlockSpec((blk, SWA_Q), lambda b, n: (cur(b, n), 0)),
                  pl.BlockSpec((blk, SWA_KV), lambda b, n: (prev(b, n), kcol)),
                  pl.BlockSpec((blk, SWA_KV), lambda b, n: (cur(b, n), kcol)),
                  pl.BlockSpec((blk, SWA_KV), lambda b, n: (prev(b, n), vcol)),
                  pl.BlockSpec((blk, SWA_KV), lambda b, n: (cur(b, n), vcol))],
        out_specs=pl.BlockSpec((blk, SWA_Q), lambda b, n: (cur(b, n), 0)),
        out_shape=jax.ShapeDtypeStruct((R, SWA_Q), BF16),
        compiler_params=_params(("parallel", "arbitrary")),
        name="swa_attention",
    )(sinks, z, z, z, z, z)


def _mla_q_body(cq_ref, g_ref, w_ref, wrot_ref, cos_ref, sin_ref, bias_ref, o_ref):
    x = cq_ref[...].astype(F32)
    ms = jnp.mean(x * x, axis=-1, keepdims=True)
    h = (x * lax.rsqrt(ms + EPS) * g_ref[...]).astype(BF16)
    a = jnp.dot(h, w_ref[...], preferred_element_type=F32)
    ar = jnp.dot(h, wrot_ref[...], preferred_element_type=F32)
    c = cos_ref[...]
    s = sin_ref[...]
    bias = bias_ref[...]
    for hh in range(MLA_HEADS):
        sl = slice(hh * LANES, (hh + 1) * LANES)
        o_ref[:, sl] = (a[:, sl] * c + ar[:, sl] * s + bias).astype(o_ref.dtype)


def _mla_q_proj(z, gain, w, wrot, cos_t, sin_t, bias_t, t_blocks, bm):
    R = z.shape[0]
    N = w.shape[1]
    table = pl.BlockSpec((bm, LANES), lambda i: (i % t_blocks, 0))
    return pl.pallas_call(
        _mla_q_body,
        grid=(R // bm,),
        in_specs=[pl.BlockSpec((bm, MLA_Q_RANK), lambda i: (i, COL_CQ // MLA_Q_RANK)),
                  pl.BlockSpec((1, MLA_Q_RANK), lambda i: (0, 0)),
                  pl.BlockSpec((MLA_Q_RANK, N), lambda i: (0, 0)),
                  pl.BlockSpec((MLA_Q_RANK, N), lambda i: (0, 0)),
                  table, table, table],
        out_specs=pl.BlockSpec((bm, N), lambda i: (i, 0)),
        out_shape=jax.ShapeDtypeStruct((R, N), BF16),
        compiler_params=_params(("parallel",)),
        name="mla_q_proj",
    )(z, gain.reshape(1, -1), w, wrot, cos_t, sin_t, bias_t)


def _mla_kv_body(ckv_ref, kr_ref, krrot_ref, g_ref, wk_ref, wv_ref, cos_ref, sin_ref, bias_ref,
                 k_ref, v_ref):
    x = ckv_ref[...].astype(F32)
    ms = jnp.mean(x * x, axis=-1, keepdims=True)
    h = (x * lax.rsqrt(ms + EPS) * g_ref[...]).astype(BF16)
    ak = jnp.dot(h, wk_ref[...], preferred_element_type=F32)
    v_ref[...] = jnp.dot(h, wv_ref[...], preferred_element_type=F32).astype(v_ref.dtype)
    rope = (kr_ref[...].astype(F32) * cos_ref[...] + krrot_ref[...].astype(F32) * sin_ref[...]
            + bias_ref[...])
    for hh in range(MLA_HEADS):
        sl = slice(hh * LANES, (hh + 1) * LANES)
        k_ref[:, sl] = (ak[:, sl] + rope).astype(k_ref.dtype)


def _mla_kv_proj(z, gain, wk, wv, cos_t, sin_t, bias_t, t_blocks, bm):
    R = z.shape[0]
    table = pl.BlockSpec((bm, LANES), lambda i: (i % t_blocks, 0))
    return pl.pallas_call(
        _mla_kv_body,
        grid=(R // bm,),
        in_specs=[pl.BlockSpec((bm, MLA_KV_RANK), lambda i: (i, COL_CKV // MLA_KV_RANK)),
                  pl.BlockSpec((bm, LANES), lambda i: (i, COL_KR // LANES)),
                  pl.BlockSpec((bm, LANES), lambda i: (i, COL_KRROT // LANES)),
                  pl.BlockSpec((1, MLA_KV_RANK), lambda i: (0, 0)),
                  pl.BlockSpec(wk.shape, lambda i: (0, 0)),
                  pl.BlockSpec(wv.shape, lambda i: (0, 0)),
                  table, table, table],
        out_specs=[pl.BlockSpec((bm, wk.shape[1]), lambda i: (i, 0)),
                   pl.BlockSpec((bm, wv.shape[1]), lambda i: (i, 0))],
        out_shape=[jax.ShapeDtypeStruct((R, wk.shape[1]), BF16),
                   jax.ShapeDtypeStruct((R, wv.shape[1]), BF16)],
        compiler_params=_params(("parallel",)),
        name="mla_kv_proj",
    )(z, z, z, gain.reshape(1, -1), wk, wv, cos_t, sin_t, bias_t)


def _lane_blocks(s):
    return [s[:, c * LANES:(c + 1) * LANES] for c in range(s.shape[1] // LANES)]


def _mla_attn_body(q_ref, k_ref, v_ref, o_ref, vext_ref, m_ref, acc_ref, *, bq, heads):
    i = pl.program_id(2)
    dv = MLA_V_DIM
    kb = MXU_COLS
    T = k_ref.shape[0]
    lane = lax.broadcasted_iota(jnp.int32, (1, LANES), 1)
    own = [(lane >= hh * dv) & (lane < (hh + 1) * dv) for hh in range(heads)]

    @pl.when(i == 0)
    def _():
        for c in range(T // bq):
            rows = slice(c * bq, (c + 1) * bq)
            vv = v_ref[rows, :]
            for hh in range(heads):
                vext_ref[hh, rows, :] = jnp.where(own[hh], vv, jnp.ones_like(vv))

    def scores(hh, start, size, row0=0):
        sl = slice(hh * LANES, (hh + 1) * LANES)
        return lax.dot_general(q_ref[row0:, sl], k_ref[pl.ds(start, size), sl],
                               (((1,), (1,)), ((), ())), preferred_element_type=F32)

    def online(hh, start, size, row0=0, mask=None):
        s = scores(hh, start, size, row0)
        if mask is not None:
            s = mask(s)
        blocks = _lane_blocks(s)
        top = blocks[0]
        for blk in blocks[1:]:
            top = jnp.maximum(top, blk)
        m = m_ref[hh, row0:, :]
        m_new = jnp.maximum(m, jnp.max(top, axis=-1, keepdims=True))
        p = jnp.concatenate([jnp.exp2(blk - m_new) for blk in blocks], axis=1).astype(BF16)
        m_ref[hh, row0:, :] = m_new
        acc_ref[hh, row0:, :] = (jnp.exp2(m - m_new) * acc_ref[hh, row0:, :]
                                 + jnp.dot(p, vext_ref[hh, pl.ds(start, size), :],
                                           preferred_element_type=F32))

    m_ref[...] = jnp.full(m_ref.shape, NEG, F32)
    acc_ref[...] = jnp.zeros(acc_ref.shape, F32)

    visible = i * bq
    big = 2 * kb
    n_big = visible // big
    left = visible - n_big * big

    def big_tile(j, carry):
        for hh in range(heads):
            online(hh, pl.multiple_of(j * big, big), big)
        return carry

    lax.fori_loop(0, n_big, big_tile, 0)

    @pl.when(left >= kb)
    def _():
        for hh in range(heads):
            online(hh, pl.multiple_of(n_big * big, kb), kb)

    @pl.when(left % kb > 0)
    def _():
        for hh in range(heads):
            online(hh, pl.multiple_of(visible - LANES, LANES), LANES)

    for k0 in range(0, bq, big):
        size = min(big, bq - k0)
        lower = (lax.broadcasted_iota(jnp.int32, (size, 1), 0)
                 >= lax.broadcasted_iota(jnp.int32, (1, size), 1))

        def causal(s, size=size, lower=lower):
            head = jnp.where(lower, s[:size], NEG)
            return head if s.shape[0] == size else jnp.concatenate([head, s[size:]], axis=0)

        for hh in range(heads):
            online(hh, pl.multiple_of(visible + k0, LANES), size, row0=k0, mask=causal)

    out = jnp.zeros((bq, LANES), F32)
    for hh in range(heads):
        acc = acc_ref[hh]
        denom = pltpu.roll(acc, dv, axis=1)
        out = jnp.where(own[hh], acc / denom, out)
    q_pos = i * bq + lax.broadcasted_iota(jnp.int32, (bq, 1), 0)
    o_ref[...] = jnp.where(q_pos >= FRAME_PAD, out, 0.0).astype(o_ref.dtype)


def _mla_attention(q, k, v, batch, T, bq):
    R = q.shape[0]
    heads = LANES // MLA_V_DIM
    groups = MLA_HEADS // heads
    t_blocks = T // bq
    wq = heads * LANES
    return pl.pallas_call(
        functools.partial(_mla_attn_body, bq=bq, heads=heads),
        grid=(batch, groups, t_blocks),
        in_specs=[pl.BlockSpec((bq, wq), lambda b, g, i: (b * t_blocks + i, g)),
                  pl.BlockSpec((T, wq), lambda b, g, i: (b, g)),
                  pl.BlockSpec((T, LANES), lambda b, g, i: (b, g))],
        out_specs=pl.BlockSpec((bq, LANES), lambda b, g, i: (b * t_blocks + i, g)),
        out_shape=jax.ShapeDtypeStruct((R, MLA_HEADS * MLA_V_DIM), BF16),
        scratch_shapes=[pltpu.VMEM((heads, T, LANES), BF16),
                        pltpu.VMEM((heads, bq, LANES), F32),
                        pltpu.VMEM((heads, bq, LANES), F32)],
        compiler_params=_params(("parallel", "parallel", "arbitrary")),
        name="mla_attention",
    )(q, k, v)


def _rec_in_body(x_ref, wq_ref, wf_ref, wi_ref, wg_ref, la_ref, lc_ref,
                 q_ref, lf_ref, i_ref, g_ref):
    x = x_ref[...]
    q = jnp.dot(x, wq_ref[...], preferred_element_type=F32)
    q_ref[...] = _silu(q).astype(q_ref.dtype)
    f = jnp.dot(x, wf_ref[...], preferred_element_type=F32) * LOG2_E
    log_sig = jnp.minimum(f, 0.0) - jnp.log2(1.0 + jnp.exp2(-jnp.abs(f)))
    y = lc_ref[...] + log_sig
    la = la_ref[...]
    gap = jnp.minimum(la - y, y - la)
    lf_ref[...] = jnp.maximum(la, y) + jnp.log2(1.0 + jnp.exp2(gap))
    i_ref[...] = jnp.dot(x, wi_ref[...], preferred_element_type=F32).astype(i_ref.dtype)
    g = jnp.dot(x, wg_ref[...], preferred_element_type=F32)
    g_ref[...] = _silu(g).astype(g_ref.dtype)


def _rec_in_proj(h, w_in, layer, log_lb, log_1m_lb):
    R, K = h.shape
    D = w_in.shape[2] // 4
    bm = _pick(R, (768, 512, 384, 256, 128))
    bn = _pick(D, (512, 256, 128))
    nb = D // bn

    def wspec(sec):
        return pl.BlockSpec((None, K, bn), lambda i, j: (layer, 0, sec * nb + j))

    ospec = pl.BlockSpec((bm, bn), lambda i, j: (i, j))
    vspec = pl.BlockSpec((1, bn), lambda i, j: (0, j))
    return pl.pallas_call(
        _rec_in_body,
        grid=(R // bm, nb),
        in_specs=[pl.BlockSpec((bm, K), lambda i, j: (i, 0)),
                  wspec(0), wspec(1), wspec(2), wspec(3), vspec, vspec],
        out_specs=[ospec, ospec, ospec, ospec],
        out_shape=[jax.ShapeDtypeStruct((R, D), BF16),
                   jax.ShapeDtypeStruct((R, D), F32),
                   jax.ShapeDtypeStruct((R, D), BF16),
                   jax.ShapeDtypeStruct((R, D), BF16)],
        compiler_params=_params(("parallel", "arbitrary")),
        name="rec_in_proj",
    )(h, w_in, w_in, w_in, w_in, log_lb.reshape(1, D), log_1m_lb.reshape(1, D))


def _segment_tail(cum, w):
    C, W = cum.shape
    if 2 * w >= F32_ROWS * 2:
        pieces = [jnp.broadcast_to(cum[r:r + 1, :], (2 * w, W))
                  for r in range(w - 1, C, 2 * w)]
        return pieces[0] if len(pieces) == 1 else jnp.concatenate(pieces, axis=0)
    tiles = cum.reshape(C // F32_ROWS, F32_ROWS, W)
    sub = lax.broadcasted_iota(jnp.int32, (1, F32_ROWS, 1), 1)
    out = None
    for r in reversed(range(w - 1, F32_ROWS, 2 * w)):
        piece = jnp.broadcast_to(tiles[:, r:r + 1, :], tiles.shape)
        out = piece if out is None else jnp.where(sub < r + w + 1, piece, out)
    return out.reshape(C, W)


def _rec_body(q_ref, lf_ref, v_ref, gs_ref, gain_ref, o_ref, state_ref):
    c = pl.program_id(1)
    C = q_ref.shape[0]
    D = REC_DIM
    contract_last = (((1,), (1,)), ((), ()))
    contract_first = (((0,), (0,)), ((), ()))

    @pl.when(c == 0)
    def _():
        state_ref[...] = jnp.zeros_like(state_ref)

    row = lax.broadcasted_iota(jnp.int32, (C, 1), 0)
    g = jnp.where(c * C + row >= FRAME_PAD, lf_ref[...], 0.0)
    kk = 1.0 - jnp.exp2(g)
    q16 = q_ref[...]
    q = q16.astype(F32)
    k16 = kk.astype(BF16)

    ti = lax.broadcasted_iota(jnp.int32, (C, C), 0)
    si = lax.broadcasted_iota(jnp.int32, (C, C), 1)
    bit_diff = jnp.where(ti > si, ti ^ si, 0)
    cum = g
    levels = []
    w = 1
    while w < C:
        upper = (row & w) != 0
        tail = _segment_tail(cum, w)
        f = jnp.exp2(jnp.where(upper, cum, tail - cum))
        pair_mask = (bit_diff >= w) & (bit_diff < 2 * w)
        levels.append(((jnp.where(upper, q, kk) * f).astype(BF16), pair_mask))
        cum = cum + jnp.where(upper, tail, 0.0)
        w *= 2
    b = cum
    same_row = ti == si

    b_last = b[C - 1:C, :]
    q_inter = (q * jnp.exp2(b)).astype(BF16)
    k_state = (kk * jnp.exp2(b_last - b)).astype(BF16)
    state_decay = jnp.exp2(b_last)

    v = v_ref[...]
    gain = gain_ref[...]
    for h in range(REC_HEADS):
        sl = slice(h * D, (h + 1) * D)
        st = state_ref[h]
        att = jnp.where(same_row,
                        lax.dot_general(q16[:, sl], k16[:, sl], contract_last,
                                        preferred_element_type=F32), 0.0)
        for qk_lvl, pair_mask in levels:
            att = jnp.where(pair_mask,
                            lax.dot_general(qk_lvl[:, sl], qk_lvl[:, sl], contract_last,
                                            preferred_element_type=F32), att)
        o = (lax.dot_general(q_inter[:, sl], st.astype(BF16), contract_last,
                             preferred_element_type=F32)
             + jnp.dot(att.astype(BF16), v[:, sl], preferred_element_type=F32))
        upd = lax.dot_general(v[:, sl], k_state[:, sl], contract_first,
                              preferred_element_type=F32)
        state_ref[h] = state_decay[:, sl] * st + upd
        ms = jnp.mean(o * o, axis=-1, keepdims=True)
        y = o * lax.rsqrt(ms + EPS) * gain * gs_ref[:, sl].astype(F32)
        o_ref[:, sl] = y.astype(o_ref.dtype)


def _rec_mixer(qs, log_f, v, gs, out_gain, batch, T):
    R, W = qs.shape
    C = REC_CHUNK
    n_chunks = T // C
    spec = pl.BlockSpec((C, W), lambda b, c: (b * n_chunks + c, 0))
    return pl.pallas_call(
        _rec_body,
        grid=(batch, n_chunks),
        in_specs=[spec, spec, spec, spec, pl.BlockSpec((1, REC_DIM), lambda b, c: (0, 0))],
        out_specs=spec,
        out_shape=jax.ShapeDtypeStruct((R, W), BF16),
        scratch_shapes=[pltpu.VMEM((REC_HEADS, REC_DIM, REC_DIM), F32)],
        compiler_params=_params(("parallel", "arbitrary")),
        name="rec_mixer",
    )(qs, log_f, v, gs, out_gain.reshape(1, REC_DIM))


def _att_in_weight(w_in):
    d = w_in.shape[0]
    half = MLA_ROPE_DIM // 2
    w_kr = w_in[:, COL_KR:COL_KR + MLA_ROPE_DIM]
    rot = jnp.concatenate([-w_kr[:, half:], w_kr[:, :half]], axis=1)
    left = jnp.zeros((d, MLA_NOPE_DIM), w_in.dtype)
    right = jnp.zeros((d, LANES - MLA_NOPE_DIM - MLA_ROPE_DIM), w_in.dtype)
    return jnp.concatenate([w_in[:, :COL_KR], left, w_kr, right, left, rot, right], axis=1)


def _mla_q_weights(w_uq):
    r = w_uq.shape[0]
    half = MLA_ROPE_DIM // 2
    w = w_uq.reshape(r, MLA_HEADS, MLA_NOPE_DIM + MLA_ROPE_DIM)
    nope, rope = w[..., :MLA_NOPE_DIM], w[..., MLA_NOPE_DIM:]
    rot = jnp.concatenate([-rope[..., half:], rope[..., :half]], axis=-1)
    fill = jnp.zeros((r, MLA_HEADS, LANES - MLA_NOPE_DIM - MLA_ROPE_DIM), w.dtype)
    plain = jnp.concatenate([nope, rope, fill], axis=-1).reshape(r, MLA_HEADS * LANES)
    rotated = jnp.concatenate([jnp.zeros_like(nope), rot, fill], axis=-1).reshape(r, MLA_HEADS * LANES)
    return plain, rotated


def _mla_kv_weights(w_ukv):
    r = w_ukv.shape[0]
    w = w_ukv.reshape(r, MLA_HEADS, MLA_NOPE_DIM + MLA_V_DIM)
    k_nope, v = w[..., :MLA_NOPE_DIM], w[..., MLA_NOPE_DIM:]
    wk = jnp.concatenate([k_nope, jnp.zeros((r, MLA_HEADS, LANES - MLA_NOPE_DIM), w.dtype)], axis=-1)
    return wk.reshape(r, MLA_HEADS * LANES), v.reshape(r, MLA_HEADS * MLA_V_DIM)


def _rope_tables(T):
    half = MLA_ROPE_DIM // 2
    inv_freq = ROPE_THETA ** (-2.0 * jnp.arange(half, dtype=F32) / MLA_ROPE_DIM)
    pos = jnp.maximum(jnp.arange(T) - FRAME_PAD, 0).astype(F32)
    ang = pos[:, None] * inv_freq[None, :]
    cos, sin = jnp.cos(ang), jnp.sin(ang)
    left = jnp.ones((T, MLA_NOPE_DIM), F32)
    right = jnp.zeros((T, LANES - MLA_NOPE_DIM - MLA_ROPE_DIM), F32)
    cos_t = jnp.concatenate([left, cos, cos, right], axis=1)
    sin_t = jnp.concatenate([0.0 * left, sin, sin, right], axis=1)
    spare = (jnp.arange(LANES) == MLA_NOPE_DIM + MLA_ROPE_DIM).astype(F32)[None, :]
    is_pad = (jnp.arange(T) < FRAME_PAD).astype(F32)[:, None]
    q_bias = jnp.broadcast_to(spare, (T, LANES))
    k_bias = NEG * is_pad * spare
    return cos_t, sin_t, q_bias, k_bias


def kernel(x, meta_tokens, mix_norm, ffn_norm, final_norm, att_w_in, att_sinks, mla_q_norm,
           mla_w_uq, mla_kv_norm, mla_w_ukv, att_w_out, rec_w_in, rec_lower_bounds, rec_out_norm,
           rec_w_out, ffn_w_up, ffn_w_gate, ffn_conv_w, ffn_conv_b, ffn_w_down):
    B, S, D = x.shape
    depth = mix_norm.shape[0]
    T = FRAME_PAD + N_META + S
    assert T % ATT_BLOCK == 0 and T % REC_CHUNK == 0
    R = B * T

    meta = jnp.broadcast_to(meta_tokens[None].astype(x.dtype), (B, N_META, D))
    hs = jnp.concatenate([jnp.zeros((B, FRAME_PAD, D), x.dtype), meta, x], axis=1).reshape(R, D)

    cos_t, sin_t, q_bias, k_bias = _rope_tables(T)
    q_scale = (MLA_NOPE_DIM + MLA_ROPE_DIM) ** -0.5 * LOG2_E
    sm = jax.nn.softmax(rec_lower_bounds.astype(F32), axis=0)
    lower = jnp.cumsum(sm.at[0].set(0.0), axis=0)
    log_lb, log_1m_lb = jnp.log(lower) * LOG2_E, jnp.log1p(-lower) * LOG2_E

    bq = _pick(T, (1408, 384, 128))
    t_blocks = T // bq

    att_w_out16, rec_w_out16 = att_w_out.astype(BF16), rec_w_out.astype(BF16)
    rec_w_in16, ffn_w_down16 = rec_w_in.astype(BF16), ffn_w_down.astype(BF16)

    h = _rmsnorm(hs, mix_norm[0], BF16)
    for layer in range(depth):
        idx = layer // 2
        if layer % 2 == 0:
            z = _matmul(h, _att_in_weight(att_w_in[idx]).astype(BF16), BF16)
            o_a = _swa_attention(z, att_sinks[idx].astype(F32), B, T // ATT_BLOCK)
            wq, wq_rot = _mla_q_weights(mla_w_uq[idx])
            wk, wv = _mla_kv_weights(mla_w_ukv[idx])
            q = _mla_q_proj(z, mla_q_norm[idx], wq.astype(BF16), wq_rot.astype(BF16),
                            cos_t * q_scale, sin_t * q_scale, q_bias, t_blocks, bq)
            k, v = _mla_kv_proj(z, mla_kv_norm[idx], wk.astype(BF16), wv.astype(BF16),
                                cos_t, sin_t, k_bias, t_blocks, bq)
            o_b = _mla_attention(q, k, v, B, T, _pick(T, (1408, 384, 128)))
            mixed = (o_a, o_b)
            w_out = att_w_out16
        else:
            qs, log_f, iv, gs = _rec_in_proj(h, rec_w_in16, idx, log_lb[idx], log_1m_lb[idx])
            mixed = _rec_mixer(qs, log_f, iv, gs, rec_out_norm[idx], B, T)
            w_out = rec_w_out16
        hs, h = _proj_residual_norm(mixed, w_out, idx, hs, ffn_norm[layer], BF16)
        g = _ffn_up(h, ffn_w_up, ffn_w_gate, ffn_conv_w, ffn_conv_b, layer)
        if layer + 1 < depth:
            hs, h = _proj_residual_norm(g, ffn_w_down16, layer, hs, mix_norm[layer + 1], BF16)
        else:
            out = _final_proj_norm(g, ffn_w_down16, layer, hs, final_norm, B, T,
                                   FRAME_PAD + N_META, x.dtype)
    return out.reshape(B, S, D)
```

```python
import functools

import jax
import jax.numpy as jnp
from jax import lax
from jax.experimental import pallas as pl
from jax.experimental.pallas import tpu as pltpu

F32 = jnp.float32
BF16 = jnp.bfloat16

N_META = 16
EPS = 1e-6
NEG = -1e30
LOG2_E = 1.4426950408889634
ATT_BLOCK = 128
SWA_HEADS = 16
SWA_KV_HEADS = 4
SWA_HEAD_DIM = 64
MLA_HEADS = 16
MLA_Q_RANK = 512
MLA_KV_RANK = 256
MLA_NOPE_DIM = 64
MLA_ROPE_DIM = 32
MLA_V_DIM = 64
ROPE_THETA = 10000.0
REC_HEADS = 16
REC_DIM = 128
CONV_WIDTH = 3

LANES = 128
MXU_COLS = 256
F32_ROWS = 8
BF16_ROWS = 16
FRAME_PAD = (-N_META) % ATT_BLOCK
REC_CHUNK = 128
VMEM_LIMIT = 56 * 1024 * 1024

SWA_Q = SWA_HEADS * SWA_HEAD_DIM
SWA_KV = SWA_KV_HEADS * SWA_HEAD_DIM
COL_QA = 0
COL_KA = SWA_Q
COL_VA = SWA_Q + SWA_KV
COL_CQ = SWA_Q + 2 * SWA_KV
COL_CKV = COL_CQ + MLA_Q_RANK
COL_KR = COL_CKV + MLA_KV_RANK
COL_KRROT = COL_KR + LANES
ATT_IN_PAD = COL_KRROT + LANES


def _pick(n, candidates):
    for c in candidates:
        if n % c == 0:
            return c
    raise ValueError(f"no block size in {candidates} divides {n}")


def _params(sem):
    return pltpu.CompilerParams(dimension_semantics=sem, vmem_limit_bytes=VMEM_LIMIT)


def _silu(x):
    return x * (1.0 / (1.0 + jnp.exp(-x)))


def _rmsnorm_body(x_ref, g_ref, o_ref):
    x = x_ref[...]
    ms = jnp.mean(x * x, axis=-1, keepdims=True)
    o_ref[...] = (x * lax.rsqrt(ms + EPS) * g_ref[...]).astype(o_ref.dtype)


def _rmsnorm(x, gain, out_dtype):
    R, D = x.shape
    bm = _pick(R, (768, 512, 384, 256, 128))
    return pl.pallas_call(
        _rmsnorm_body,
        grid=(R // bm,),
        in_specs=[pl.BlockSpec((bm, D), lambda i: (i, 0)),
                  pl.BlockSpec((1, D), lambda i: (0, 0))],
        out_specs=pl.BlockSpec((bm, D), lambda i: (i, 0)),
        out_shape=jax.ShapeDtypeStruct((R, D), out_dtype),
        compiler_params=_params(("parallel",)),
        name="rmsnorm",
    )(x, gain.reshape(1, D))


def _matmul_body(x_ref, w_ref, o_ref):
    o_ref[...] = jnp.dot(x_ref[...], w_ref[...], preferred_element_type=F32).astype(o_ref.dtype)


def _matmul(x, w, out_dtype):
    R, K = x.shape
    N = w.shape[1]
    bm = _pick(R, (1408, 768, 512, 384, 256, 128))
    bn = _pick(N, (1280, 1024, 512, 256, 128))
    return pl.pallas_call(
        _matmul_body,
        grid=(R // bm, N // bn),
        in_specs=[pl.BlockSpec((bm, K), lambda i, j: (i, 0)),
                  pl.BlockSpec((K, bn), lambda i, j: (0, j))],
        out_specs=pl.BlockSpec((bm, bn), lambda i, j: (i, j)),
        out_shape=jax.ShapeDtypeStruct((R, N), out_dtype),
        compiler_params=_params(("parallel", "arbitrary")),
        name="att_in_proj",
    )(x, w)


def _proj_res_body(*refs, n_x=1):
    x_refs, (w_ref, res_ref, g_ref, *out_refs) = refs[:n_x], refs[n_x:]
    *hs_refs, h_ref = out_refs
    y = res_ref[...]
    k0 = 0
    for x_ref in x_refs:
        k1 = k0 + x_ref.shape[1]
        y = y + jnp.dot(x_ref[...], w_ref[k0:k1, :], preferred_element_type=F32)
        k0 = k1
    for hs_ref in hs_refs:
        hs_ref[...] = y
    ms = jnp.mean(y * y, axis=-1, keepdims=True)
    h_ref[...] = (y * lax.rsqrt(ms + EPS) * g_ref[...]).astype(h_ref.dtype)


def _proj_residual_norm(x, w_stack, layer, res, gain, h_dtype, emit_residual=True):
    xs = x if isinstance(x, tuple) else (x,)
    R = xs[0].shape[0]
    K, N = w_stack.shape[1:]
    assert sum(part.shape[1] for part in xs) == K
    bm = _pick(R, (384, 256, 128))
    row_spec = pl.BlockSpec((bm, N), lambda i: (i, 0))
    out_specs = [row_spec]
    out_shape = [jax.ShapeDtypeStruct((R, N), h_dtype)]
    if emit_residual:
        out_specs = [row_spec] + out_specs
        out_shape = [jax.ShapeDtypeStruct((R, N), F32)] + out_shape
    return pl.pallas_call(
        functools.partial(_proj_res_body, n_x=len(xs)),
        grid=(R // bm,),
        in_specs=[pl.BlockSpec((bm, part.shape[1]), lambda i: (i, 0)) for part in xs]
                 + [pl.BlockSpec((None, K, N), lambda i: (layer, 0, 0),
                                 pipeline_mode=pl.Buffered(1)),
                    row_spec,
                    pl.BlockSpec((1, N), lambda i: (0, 0))],
        out_specs=out_specs,
        out_shape=out_shape,
        compiler_params=_params(("parallel",)),
        name="proj_residual_norm",
    )(*xs, w_stack, res, gain.reshape(1, N))


def _final_proj_norm(x, w_stack, layer, res, gain, batch, T, skip, out_dtype):
    K = x.shape[1]
    N = w_stack.shape[2]
    S = T - skip
    bm = _pick(S, (256, 128))
    nb = S // bm

    def frame_row(b, j):
        return pl.multiple_of(b * T + skip + j * bm, LANES)

    return pl.pallas_call(
        _proj_res_body,
        grid=(batch, nb),
        in_specs=[pl.BlockSpec((pl.Element(bm), pl.Element(K)), lambda b, j: (frame_row(b, j), 0)),
                  pl.BlockSpec((None, K, N), lambda b, j: (layer, 0, 0),
                               pipeline_mode=pl.Buffered(1)),
                  pl.BlockSpec((pl.Element(bm), pl.Element(N)), lambda b, j: (frame_row(b, j), 0)),
                  pl.BlockSpec((1, N), lambda b, j: (0, 0))],
        out_specs=[pl.BlockSpec((bm, N), lambda b, j: (b * nb + j, 0))],
        out_shape=[jax.ShapeDtypeStruct((batch * S, N), out_dtype)],
        compiler_params=_params(("parallel", "parallel")),
        name="final_proj_norm",
    )(x, w_stack, res, gain.reshape(1, N))[0]


def _ffn_up_body(x_ref, halo_ref, wu_ref, wg_ref, cw_ref, cb_ref, o_ref, wu16_ref, wg16_ref, a_ref):
    bm = x_ref.shape[0]
    halo = halo_ref.shape[0]

    @pl.when(pl.program_id(1) == 0)
    def _():
        wu16_ref[...] = wu_ref[...].astype(BF16)
        wg16_ref[...] = wg_ref[...].astype(BF16)

    wg = wg16_ref[...]
    a_ref[0:halo, :] = jnp.dot(halo_ref[...], wg, preferred_element_type=F32)
    a_ref[halo:, :] = jnp.dot(x_ref[...], wg, preferred_element_type=F32)
    u = jnp.dot(x_ref[...], wu16_ref[...], preferred_element_type=F32)
    cw = cw_ref[...]
    a = cb_ref[...] + cw[CONV_WIDTH - 1:CONV_WIDTH, :] * a_ref[halo:, :]
    for j in range(CONV_WIDTH - 1):
        back = CONV_WIDTH - 1 - j
        a = a + cw[j:j + 1, :] * a_ref[pl.ds(halo - back, bm), :]
    o_ref[...] = (_silu(a) * u).astype(o_ref.dtype)


def _ffn_up(h, w_up, w_gate, conv_w, conv_b, layer):
    R, K = h.shape
    N = w_up.shape[2]
    bm = _pick(R, (1408, 768, 512, 384, 256, 128))
    bn = _pick(N, (512, 256, 128))
    halo = BF16_ROWS
    hb = bm // halo
    return pl.pallas_call(
        _ffn_up_body,
        grid=(N // bn, R // bm),
        in_specs=[pl.BlockSpec((bm, K), lambda j, i: (i, 0)),
                  pl.BlockSpec((halo, K), lambda j, i: (jnp.maximum(i * hb - 1, 0), 0)),
                  pl.BlockSpec((None, K, bn), lambda j, i: (layer, 0, j)),
                  pl.BlockSpec((None, K, bn), lambda j, i: (layer, 0, j)),
                  pl.BlockSpec((None, CONV_WIDTH, bn), lambda j, i: (layer, 0, j)),
                  pl.BlockSpec((None, 1, bn), lambda j, i: (layer, 0, j))],
        out_specs=pl.BlockSpec((bm, bn), lambda j, i: (i, j)),
        out_shape=jax.ShapeDtypeStruct((R, N), BF16),
        scratch_shapes=[pltpu.VMEM((K, bn), BF16),
                        pltpu.VMEM((K, bn), BF16),
                        pltpu.VMEM((bm + halo, bn), F32)],
        compiler_params=_params(("parallel", "arbitrary")),
        name="ffn_up",
    )(h, h, w_up, w_gate, conv_w, conv_b.reshape(conv_b.shape[0], 1, N))


def _swa_body(sink_ref, q_ref, kp_ref, kc_ref, vp_ref, vc_ref, o_ref):
    n = pl.program_id(1)
    blk = q_ref.shape[0]
    dh = SWA_HEAD_DIM
    group = SWA_HEADS // SWA_KV_HEADS
    qi = lax.broadcasted_iota(jnp.int32, (blk, 2 * blk), 0)
    kj = lax.broadcasted_iota(jnp.int32, (blk, 2 * blk), 1)
    rel = qi + blk - kj
    key_pos = n * blk + kj - blk
    mask = (rel >= 0) & (rel < blk) & (key_pos >= FRAME_PAD)
    for g in range(SWA_KV_HEADS):
        ks = slice(g * dh, (g + 1) * dh)
        k = jnp.concatenate([kp_ref[:, ks], kc_ref[:, ks]], axis=0)
        v = jnp.concatenate([vp_ref[:, ks], vc_ref[:, ks]], axis=0)
        for hh in range(group):
            h = g * group + hh
            hs = slice(h * dh, (h + 1) * dh)
            q = q_ref[:, hs] * (dh ** -0.5)
            s = lax.dot_general(q, k, (((1,), (1,)), ((), ())), preferred_element_type=F32)
            s = jnp.where(mask, s, NEG)
            sink = sink_ref[h]
            m = jnp.maximum(jnp.max(s, axis=-1, keepdims=True), sink)
            p = jnp.exp(s - m)
            den = jnp.sum(p, axis=-1, keepdims=True) + jnp.exp(sink - m)
            o = jnp.dot(p.astype(BF16), v, preferred_element_type=F32) / den
            o_ref[:, hs] = o.astype(o_ref.dtype)


def _swa_attention(z, sinks, batch, n_blocks):
    R = z.shape[0]
    blk = ATT_BLOCK
    kcol = COL_KA // SWA_KV
    vcol = COL_VA // SWA_KV

    def cur(b, n):
        return b * n_blocks + n

    def prev(b, n):
        return b * n_blocks + jnp.maximum(n - 1, 0)

    return pl.pallas_call(
        _swa_body,
        grid=(batch, n_blocks),
        in_specs=[pl.BlockSpec(memory_space=pltpu.SMEM),
                  pl.BlockSpec((blk, SWA_Q), lambda b, n: (cur(b, n), 0)),
                  pl.BlockSpec((blk, SWA_KV), lambda b, n: (prev(b, n), kcol)),
                  pl.BlockSpec((blk, SWA_KV), lambda b, n: (cur(b, n), kcol)),
                  pl.BlockSpec((blk, SWA_KV), lambda b, n: (prev(b, n), vcol)),
                  pl.BlockSpec((blk, SWA_KV), lambda b, n: (cur(b, n), vcol))],
        out_specs=pl.BlockSpec((blk, SWA_Q), lambda b, n: (cur(b, n), 0)),
        out_shape=jax.ShapeDtypeStruct((R, SWA_Q), BF16),
        compiler_params=_params(("parallel", "arbitrary")),
        name="swa_attention",
    )(sinks, z, z, z, z, z)


def _mla_q_body(cq_ref, g_ref, w_ref, wrot_ref, cos_ref, sin_ref, bias_ref, o_ref):
    x = cq_ref[...].astype(F32)
    ms = jnp.mean(x * x, axis=-1, keepdims=True)
    h = (x * lax.rsqrt(ms + EPS) * g_ref[...]).astype(BF16)
    a = jnp.dot(h, w_ref[...], preferred_element_type=F32)
    ar = jnp.dot(h, wrot_ref[...], preferred_element_type=F32)
    c = cos_ref[...]
    s = sin_ref[...]
    bias = bias_ref[...]
    for hh in range(MLA_HEADS):
        sl = slice(hh * LANES, (hh + 1) * LANES)
        o_ref[:, sl] = (a[:, sl] * c + ar[:, sl] * s + bias).astype(o_ref.dtype)


def _mla_q_proj(z, gain, w, wrot, cos_t, sin_t, bias_t, t_blocks, bm):
    R = z.shape[0]
    N = w.shape[1]
    table = pl.BlockSpec((bm, LANES), lambda i: (i % t_blocks, 0))
    return pl.pallas_call(
        _mla_q_body,
        grid=(R // bm,),
        in_specs=[pl.BlockSpec((bm, MLA_Q_RANK), lambda i: (i, COL_CQ // MLA_Q_RANK)),
                  pl.BlockSpec((1, MLA_Q_RANK), lambda i: (0, 0)),
                  pl.BlockSpec((MLA_Q_RANK, N), lambda i: (0, 0)),
                  pl.BlockSpec((MLA_Q_RANK, N), lambda i: (0, 0)),
                  table, table, table],
        out_specs=pl.BlockSpec((bm, N), lambda i: (i, 0)),
        out_shape=jax.ShapeDtypeStruct((R, N), BF16),
        compiler_params=_params(("parallel",)),
        name="mla_q_proj",
    )(z, gain.reshape(1, -1), w, wrot, cos_t, sin_t, bias_t)


def _mla_kv_body(ckv_ref, kr_ref, krrot_ref, g_ref, wk_ref, wv_ref, cos_ref, sin_ref, bias_ref,
                 k_ref, v_ref):
    x = ckv_ref[...].astype(F32)
    ms = jnp.mean(x * x, axis=-1, keepdims=True)
    h = (x * lax.rsqrt(ms + EPS) * g_ref[...]).astype(BF16)
    ak = jnp.dot(h, wk_ref[...], preferred_element_type=F32)
    v_ref[...] = jnp.dot(h, wv_ref[...], preferred_element_type=F32).astype(v_ref.dtype)
    rope = (kr_ref[...].astype(F32) * cos_ref[...] + krrot_ref[...].astype(F32) * sin_ref[...]
            + bias_ref[...])
    for hh in range(MLA_HEADS):
        sl = slice(hh * LANES, (hh + 1) * LANES)
        k_ref[:, sl] = (ak[:, sl] + rope).astype(k_ref.dtype)


def _mla_kv_proj(z, gain, wk, wv, cos_t, sin_t, bias_t, t_blocks, bm):
    R = z.shape[0]
    table = pl.BlockSpec((bm, LANES), lambda i: (i % t_blocks, 0))
    return pl.pallas_call(
        _mla_kv_body,
        grid=(R // bm,),
        in_specs=[pl.BlockSpec((bm, MLA_KV_RANK), lambda i: (i, COL_CKV // MLA_KV_RANK)),
                  pl.BlockSpec((bm, LANES), lambda i: (i, COL_KR // LANES)),
                  pl.BlockSpec((bm, LANES), lambda i: (i, COL_KRROT // LANES)),
                  pl.BlockSpec((1, MLA_KV_RANK), lambda i: (0, 0)),
                  pl.BlockSpec(wk.shape, lambda i: (0, 0)),
                  pl.BlockSpec(wv.shape, lambda i: (0, 0)),
                  table, table, table],
        out_specs=[pl.BlockSpec((bm, wk.shape[1]), lambda i: (i, 0)),
                   pl.BlockSpec((bm, wv.shape[1]), lambda i: (i, 0))],
        out_shape=[jax.ShapeDtypeStruct((R, wk.shape[1]), BF16),
                   jax.ShapeDtypeStruct((R, wv.shape[1]), BF16)],
        compiler_params=_params(("parallel",)),
        name="mla_kv_proj",
    )(z, z, z, gain.reshape(1, -1), wk, wv, cos_t, sin_t, bias_t)


def _lane_blocks(s):
    return [s[:, c * LANES:(c + 1) * LANES] for c in range(s.shape[1] // LANES)]


def _mla_attn_body(q_ref, k_ref, v_ref, o_ref, vext_ref, m_ref, acc_ref, *, bq, heads):
    i = pl.program_id(2)
    dv = MLA_V_DIM
    kb = MXU_COLS
    T = k_ref.shape[0]
    lane = lax.broadcasted_iota(jnp.int32, (1, LANES), 1)
    own = [(lane >= hh * dv) & (lane < (hh + 1) * dv) for hh in range(heads)]

    @pl.when(i == 0)
    def _():
        for c in range(T // bq):
            rows = slice(c * bq, (c + 1) * bq)
            vv = v_ref[rows, :]
            for hh in range(heads):
                vext_ref[hh, rows, :] = jnp.where(own[hh], vv, jnp.ones_like(vv))

    def scores(hh, start, size, row0=0):
        sl = slice(hh * LANES, (hh + 1) * LANES)
        return lax.dot_general(q_ref[row0:, sl], k_ref[pl.ds(start, size), sl],
                               (((1,), (1,)), ((), ())), preferred_element_type=F32)

    def online(hh, start, size, row0=0, mask=None):
        s = scores(hh, start, size, row0)
        if mask is not None:
            s = mask(s)
        blocks = _lane_blocks(s)
        top = blocks[0]
        for blk in blocks[1:]:
            top = jnp.maximum(top, blk)
        m = m_ref[hh, row0:, :]
        m_new = jnp.maximum(m, jnp.max(top, axis=-1, keepdims=True))
        p = jnp.concatenate([jnp.exp2(blk - m_new) for blk in blocks], axis=1).astype(BF16)
        m_ref[hh, row0:, :] = m_new
        acc_ref[hh, row0:, :] = (jnp.exp2(m - m_new) * acc_ref[hh, row0:, :]
                                 + jnp.dot(p, vext_ref[hh, pl.ds(start, size), :],
                                           preferred_element_type=F32))

    m_ref[...] = jnp.full(m_ref.shape, NEG, F32)
    acc_ref[...] = jnp.zeros(acc_ref.shape, F32)

    visible = i * bq
    big = 2 * kb
    n_big = visible // big
    left = visible - n_big * big

    def big_tile(j, carry):
        for hh in range(heads):
            online(hh, pl.multiple_of(j * big, big), big)
        return carry

    lax.fori_loop(0, n_big, big_tile, 0)

    @pl.when(left >= kb)
    def _():
        for hh in range(heads):
            online(hh, pl.multiple_of(n_big * big, kb), kb)

    @pl.when(left % kb > 0)
    def _():
        for hh in range(heads):
            online(hh, pl.multiple_of(visible - LANES, LANES), LANES)

    for k0 in range(0, bq, big):
        size = min(big, bq - k0)
        lower = (lax.broadcasted_iota(jnp.int32, (size, 1), 0)
                 >= lax.broadcasted_iota(jnp.int32, (1, size), 1))

        def causal(s, size=size, lower=lower):
            head = jnp.where(lower, s[:size], NEG)
            return head if s.shape[0] == size else jnp.concatenate([head, s[size:]], axis=0)

        for hh in range(heads):
            online(hh, pl.multiple_of(visible + k0, LANES), size, row0=k0, mask=causal)

    out = jnp.zeros((bq, LANES), F32)
    for hh in range(heads):
        acc = acc_ref[hh]
        denom = pltpu.roll(acc, dv, axis=1)
        out = jnp.where(own[hh], acc / denom, out)
    q_pos = i * bq + lax.broadcasted_iota(jnp.int32, (bq, 1), 0)
    o_ref[...] = jnp.where(q_pos >= FRAME_PAD, out, 0.0).astype(o_ref.dtype)


def _mla_attention(q, k, v, batch, T, bq):
    R = q.shape[0]
    heads = LANES // MLA_V_DIM
    groups = MLA_HEADS // heads
    t_blocks = T // bq
    wq = heads * LANES
    return pl.pallas_call(
        functools.partial(_mla_attn_body, bq=bq, heads=heads),
        grid=(batch, groups, t_blocks),
        in_specs=[pl.BlockSpec((bq, wq), lambda b, g, i: (b * t_blocks + i, g)),
                  pl.BlockSpec((T, wq), lambda b, g, i: (b, g)),
                  pl.BlockSpec((T, LANES), lambda b, g, i: (b, g))],
        out_specs=pl.BlockSpec((bq, LANES), lambda b, g, i: (b * t_blocks + i, g)),
        out_shape=jax.ShapeDtypeStruct((R, MLA_HEADS * MLA_V_DIM), BF16),
        scratch_shapes=[pltpu.VMEM((heads, T, LANES), BF16),
                        pltpu.VMEM((heads, bq, LANES), F32),
                        pltpu.VMEM((heads, bq, LANES), F32)],
        compiler_params=_params(("parallel", "parallel", "arbitrary")),
        name="mla_attention",
    )(q, k, v)


def _rec_in_body(x_ref, wq_ref, wf_ref, wi_ref, wg_ref, la_ref, lc_ref,
                 q_ref, lf_ref, i_ref, g_ref):
    x = x_ref[...]
    q = jnp.dot(x, wq_ref[...], preferred_element_type=F32)
    q_ref[...] = _silu(q).astype(q_ref.dtype)
    f = jnp.dot(x, wf_ref[...], preferred_element_type=F32) * LOG2_E
    log_sig = jnp.minimum(f, 0.0) - jnp.log2(1.0 + jnp.exp2(-jnp.abs(f)))
    y = lc_ref[...] + log_sig
    la = la_ref[...]
    gap = jnp.minimum(la - y, y - la)
    lf_ref[...] = jnp.maximum(la, y) + jnp.log2(1.0 + jnp.exp2(gap))
    i_ref[...] = jnp.dot(x, wi_ref[...], preferred_element_type=F32).astype(i_ref.dtype)
    g = jnp.dot(x, wg_ref[...], preferred_element_type=F32)
    g_ref[...] = _silu(g).astype(g_ref.dtype)


def _rec_in_proj(h, w_in, layer, log_lb, log_1m_lb):
    R, K = h.shape
    D = w_in.shape[2] // 4
    bm = _pick(R, (768, 512, 384, 256, 128))
    bn = _pick(D, (512, 256, 128))
    nb = D // bn

    def wspec(sec):
        return pl.BlockSpec((None, K, bn), lambda i, j: (layer, 0, sec * nb + j))

    ospec = pl.BlockSpec((bm, bn), lambda i, j: (i, j))
    vspec = pl.BlockSpec((1, bn), lambda i, j: (0, j))
    return pl.pallas_call(
        _rec_in_body,
        grid=(R // bm, nb),
        in_specs=[pl.BlockSpec((bm, K), lambda i, j: (i, 0)),
                  wspec(0), wspec(1), wspec(2), wspec(3), vspec, vspec],
        out_specs=[ospec, ospec, ospec, ospec],
        out_shape=[jax.ShapeDtypeStruct((R, D), BF16),
                   jax.ShapeDtypeStruct((R, D), F32),
                   jax.ShapeDtypeStruct((R, D), BF16),
                   jax.ShapeDtypeStruct((R, D), BF16)],
        compiler_params=_params(("parallel", "arbitrary")),
        name="rec_in_proj",
    )(h, w_in, w_in, w_in, w_in, log_lb.reshape(1, D), log_1m_lb.reshape(1, D))


def _segment_tail(cum, w):
    C, W = cum.shape
    if 2 * w >= F32_ROWS * 2:
        pieces = [jnp.broadcast_to(cum[r:r + 1, :], (2 * w, W))
                  for r in range(w - 1, C, 2 * w)]
        return pieces[0] if len(pieces) == 1 else jnp.concatenate(pieces, axis=0)
    tiles = cum.reshape(C // F32_ROWS, F32_ROWS, W)
    sub = lax.broadcasted_iota(jnp.int32, (1, F32_ROWS, 1), 1)
    out = None
    for r in reversed(range(w - 1, F32_ROWS, 2 * w)):
        piece = jnp.broadcast_to(tiles[:, r:r + 1, :], tiles.shape)
        out = piece if out is None else jnp.where(sub < r + w + 1, piece, out)
    return out.reshape(C, W)


def _rec_body(q_ref, lf_ref, v_ref, gs_ref, gain_ref, o_ref, state_ref):
    c = pl.program_id(1)
    C = q_ref.shape[0]
    D = REC_DIM
    contract_last = (((1,), (1,)), ((), ()))
    contract_first = (((0,), (0,)), ((), ()))

    @pl.when(c == 0)
    def _():
        state_ref[...] = jnp.zeros_like(state_ref)

    row = lax.broadcasted_iota(jnp.int32, (C, 1), 0)
    g = jnp.where(c * C + row >= FRAME_PAD, lf_ref[...], 0.0)
    kk = 1.0 - jnp.exp2(g)
    q16 = q_ref[...]
    q = q16.astype(F32)
    k16 = kk.astype(BF16)

    ti = lax.broadcasted_iota(jnp.int32, (C, C), 0)
    si = lax.broadcasted_iota(jnp.int32, (C, C), 1)
    bit_diff = jnp.where(ti > si, ti ^ si, 0)
    cum = g
    levels = []
    w = 1
    while w < C:
        upper = (row & w) != 0
        tail = _segment_tail(cum, w)
        f = jnp.exp2(jnp.where(upper, cum, tail - cum))
        pair_mask = (bit_diff >= w) & (bit_diff < 2 * w)
        levels.append(((jnp.where(upper, q, kk) * f).astype(BF16), pair_mask))
        cum = cum + jnp.where(upper, tail, 0.0)
        w *= 2
    b = cum
    same_row = ti == si

    b_last = b[C - 1:C, :]
    q_inter = (q * jnp.exp2(b)).astype(BF16)
    k_state = (kk * jnp.exp2(b_last - b)).astype(BF16)
    state_decay = jnp.exp2(b_last)

    v = v_ref[...]
    gain = gain_ref[...]
    for h in range(REC_HEADS):
        sl = slice(h * D, (h + 1) * D)
        st = state_ref[h]
        att = jnp.where(same_row,
                        lax.dot_general(q16[:, sl], k16[:, sl], contract_last,
                                        preferred_element_type=F32), 0.0)
        for qk_lvl, pair_mask in levels:
            att = jnp.where(pair_mask,
                            lax.dot_general(qk_lvl[:, sl], qk_lvl[:, sl], contract_last,
                                            preferred_element_type=F32), att)
        o = (lax.dot_general(q_inter[:, sl], st.astype(BF16), contract_last,
                             preferred_element_type=F32)
             + jnp.dot(att.astype(BF16), v[:, sl], preferred_element_type=F32))
        upd = lax.dot_general(v[:, sl], k_state[:, sl], contract_first,
                              preferred_element_type=F32)
        state_ref[h] = state_decay[:, sl] * st + upd
        ms = jnp.mean(o * o, axis=-1, keepdims=True)
        y = o * lax.rsqrt(ms + EPS) * gain * gs_ref[:, sl].astype(F32)
        o_ref[:, sl] = y.astype(o_ref.dtype)


def _rec_mixer(qs, log_f, v, gs, out_gain, batch, T):
    R, W = qs.shape
    C = REC_CHUNK
    n_chunks = T // C
    spec = pl.BlockSpec((C, W), lambda b, c: (b * n_chunks + c, 0))
    return pl.pallas_call(
        _rec_body,
        grid=(batch, n_chunks),
        in_specs=[spec, spec, spec, spec, pl.BlockSpec((1, REC_DIM), lambda b, c: (0, 0))],
        out_specs=spec,
        out_shape=jax.ShapeDtypeStruct((R, W), BF16),
        scratch_shapes=[pltpu.VMEM((REC_HEADS, REC_DIM, REC_DIM), F32)],
        compiler_params=_params(("parallel", "arbitrary")),
        name="rec_mixer",
    )(qs, log_f, v, gs, out_gain.reshape(1, REC_DIM))


def _att_in_weight(w_in):
    d = w_in.shape[0]
    half = MLA_ROPE_DIM // 2
    w_kr = w_in[:, COL_KR:COL_KR + MLA_ROPE_DIM]
    rot = jnp.concatenate([-w_kr[:, half:], w_kr[:, :half]], axis=1)
    left = jnp.zeros((d, MLA_NOPE_DIM), w_in.dtype)
    right = jnp.zeros((d, LANES - MLA_NOPE_DIM - MLA_ROPE_DIM), w_in.dtype)
    return jnp.concatenate([w_in[:, :COL_KR], left, w_kr, right, left, rot, right], axis=1)


def _mla_q_weights(w_uq):
    r = w_uq.shape[0]
    half = MLA_ROPE_DIM // 2
    w = w_uq.reshape(r, MLA_HEADS, MLA_NOPE_DIM + MLA_ROPE_DIM)
    nope, rope = w[..., :MLA_NOPE_DIM], w[..., MLA_NOPE_DIM:]
    rot = jnp.concatenate([-rope[..., half:], rope[..., :half]], axis=-1)
    fill = jnp.zeros((r, MLA_HEADS, LANES - MLA_NOPE_DIM - MLA_ROPE_DIM), w.dtype)
    plain = jnp.concatenate([nope, rope, fill], axis=-1).reshape(r, MLA_HEADS * LANES)
    rotated = jnp.concatenate([jnp.zeros_like(nope), rot, fill], axis=-1).reshape(r, MLA_HEADS * LANES)
    return plain, rotated


def _mla_kv_weights(w_ukv):
    r = w_ukv.shape[0]
    w = w_ukv.reshape(r, MLA_HEADS, MLA_NOPE_DIM + MLA_V_DIM)
    k_nope, v = w[..., :MLA_NOPE_DIM], w[..., MLA_NOPE_DIM:]
    wk = jnp.concatenate([k_nope, jnp.zeros((r, MLA_HEADS, LANES - MLA_NOPE_DIM), w.dtype)], axis=-1)
    return wk.reshape(r, MLA_HEADS * LANES), v.reshape(r, MLA_HEADS * MLA_V_DIM)


def _rope_tables(T):
    half = MLA_ROPE_DIM // 2
    inv_freq = ROPE_THETA ** (-2.0 * jnp.arange(half, dtype=F32) / MLA_ROPE_DIM)
    pos = jnp.maximum(jnp.arange(T) - FRAME_PAD, 0).astype(F32)
    ang = pos[:, None] * inv_freq[None, :]
    cos, sin = jnp.cos(ang), jnp.sin(ang)
    left = jnp.ones((T, MLA_NOPE_DIM), F32)
    right = jnp.zeros((T, LANES - MLA_NOPE_DIM - MLA_ROPE_DIM), F32)
    cos_t = jnp.concatenate([left, cos, cos, right], axis=1)
    sin_t = jnp.concatenate([0.0 * left, sin, sin, right], axis=1)
    spare = (jnp.arange(LANES) == MLA_NOPE_DIM + MLA_ROPE_DIM).astype(F32)[None, :]
    is_pad = (jnp.arange(T) < FRAME_PAD).astype(F32)[:, None]
    q_bias = jnp.broadcast_to(spare, (T, LANES))
    k_bias = NEG * is_pad * spare
    return cos_t, sin_t, q_bias, k_bias


def kernel(x, meta_tokens, mix_norm, ffn_norm, final_norm, att_w_in, att_sinks, mla_q_norm,
           mla_w_uq, mla_kv_norm, mla_w_ukv, att_w_out, rec_w_in, rec_lower_bounds, rec_out_norm,
           rec_w_out, ffn_w_up, ffn_w_gate, ffn_conv_w, ffn_conv_b, ffn_w_down):
    B, S, D = x.shape
    depth = mix_norm.shape[0]
    T = FRAME_PAD + N_META + S
    assert T % ATT_BLOCK == 0 and T % REC_CHUNK == 0
    R = B * T

    meta = jnp.broadcast_to(meta_tokens[None].astype(x.dtype), (B, N_META, D))
    hs = jnp.concatenate([jnp.zeros((B, FRAME_PAD, D), x.dtype), meta, x], axis=1).reshape(R, D)

    cos_t, sin_t, q_bias, k_bias = _rope_tables(T)
    q_scale = (MLA_NOPE_DIM + MLA_ROPE_DIM) ** -0.5 * LOG2_E
    sm = jax.nn.softmax(rec_lower_bounds.astype(F32), axis=0)
    lower = jnp.cumsum(sm.at[0].set(0.0), axis=0)
    log_lb, log_1m_lb = jnp.log(lower) * LOG2_E, jnp.log1p(-lower) * LOG2_E

    bq = _pick(T, (1408, 384, 128))
    t_blocks = T // bq

    att_w_out16, rec_w_out16 = att_w_out.astype(BF16), rec_w_out.astype(BF16)
    rec_w_in16, ffn_w_down16 = rec_w_in.astype(BF16), ffn_w_down.astype(BF16)

    h = _rmsnorm(hs, mix_norm[0], BF16)
    for layer in range(depth):
        idx = layer // 2
        if layer % 2 == 0:
            z = _matmul(h, _att_in_weight(att_w_in[idx]).astype(BF16), BF16)
            o_a = _swa_attention(z, att_sinks[idx].astype(F32), B, T // ATT_BLOCK)
            wq, wq_rot = _mla_q_weights(mla_w_uq[idx])
            wk, wv = _mla_kv_weights(mla_w_ukv[idx])
            q = _mla_q_proj(z, mla_q_norm[idx], wq.astype(BF16), wq_rot.astype(BF16),
                            cos_t * q_scale, sin_t * q_scale, q_bias, t_blocks, bq)
            k, v = _mla_kv_proj(z, mla_kv_norm[idx], wk.astype(BF16), wv.astype(BF16),
                                cos_t, sin_t, k_bias, t_blocks, bq)
            o_b = _mla_attention(q, k, v, B, T, _pick(T, (1408, 384, 128)))
            mixed = (o_a, o_b)
            w_out = att_w_out16
        else:
            qs, log_f, iv, gs = _rec_in_proj(h, rec_w_in16, idx, log_lb[idx], log_1m_lb[idx])
            mixed = _rec_mixer(qs, log_f, iv, gs, rec_out_norm[idx], B, T)
            w_out = rec_w_out16
        hs, h = _proj_residual_norm(mixed, w_out, idx, hs, ffn_norm[layer], BF16)
        g = _ffn_up(h, ffn_w_up, ffn_w_gate, ffn_conv_w, ffn_conv_b, layer)
        if layer + 1 < depth:
            hs, h = _proj_residual_norm(g, ffn_w_down16, layer, hs, mix_norm[layer + 1], BF16)
        else:
            out = _final_proj_norm(g, ffn_w_down16, layer, hs, final_norm, B, T,
                                   FRAME_PAD + N_META, x.dtype)
    return out.reshape(B, S, D)
```
